```python
import math
import jax, jax.numpy as jnp
from jax import lax
import numpy as np

D_MODEL = 2048
BATCH = 4
SEQ = 4096
DEPTH = 2

HEAD_DIM = 128
N_HEADS = D_MODEL // HEAD_DIM
MOBA_HEADS = N_HEADS // 4
DIFF_HEADS = N_HEADS // 4
NSA_HEADS = N_HEADS - MOBA_HEADS - DIFF_HEADS
MIX_WIDTH = N_HEADS * HEAD_DIM

MOBA_BLOCK = 256
MOBA_TOPK = 3
MOBA_QCHUNK = 32

DIFF_QK_DIM = HEAD_DIM // 2
DIFF_QBLOCK = 128
DIFF_EPS = 1e-5

NSA_KV_HEADS = 2
NSA_GROUP = NSA_HEADS // NSA_KV_HEADS
NSA_CMP_LEN = 32
NSA_CMP_STRIDE = 16
NSA_SEL_BLOCK = 64
NSA_SEL_TOPK = 16
NSA_SEL_QCHUNK = 32
NSA_WINDOW = 512
NSA_WIN_QBLOCK = 128
NSA_BRANCHES = 3
NSA_FORCE = 1e6

REL_BUCKETS = 32
REL_MAX_EXACT = 16
REL_MAX_DIST = 128

PEER_HEADS = 8
PEER_NKEYS = 128
PEER_EXPERTS = PEER_NKEYS * PEER_NKEYS
PEER_KEY_HALF = 128
PEER_TOPK = 16
PEER_TCHUNK = 128

NORM_EPS = 1e-6
NEG_INF = -1e30

IN_SPLITS = (
    MOBA_HEADS * HEAD_DIM, MOBA_HEADS * HEAD_DIM, MOBA_HEADS * HEAD_DIM,
    DIFF_HEADS * HEAD_DIM, DIFF_HEADS * HEAD_DIM, DIFF_HEADS * HEAD_DIM,
    NSA_HEADS * HEAD_DIM,
    NSA_KV_HEADS * HEAD_DIM, NSA_KV_HEADS * HEAD_DIM,
    NSA_KV_HEADS * HEAD_DIM, NSA_KV_HEADS * HEAD_DIM,
    NSA_KV_HEADS * HEAD_DIM, NSA_KV_HEADS * HEAD_DIM,
    NSA_HEADS * NSA_BRANCHES,
)
IN_WIDTH = sum(IN_SPLITS)

kernel_name = 'hybrid_moba_diff_nsa_peer'


def rms_norm(x, g, eps=NORM_EPS):
    xf = x.astype(jnp.float32)
    y = xf * lax.rsqrt(jnp.mean(xf * xf, axis=-1, keepdims=True) + eps)
    return (y * g.astype(jnp.float32)).astype(x.dtype)


def t5_bucket(dist):
    n = jnp.maximum(dist, 0)
    nf = jnp.maximum(n, REL_MAX_EXACT).astype(jnp.float32)
    large = REL_MAX_EXACT + (jnp.log(nf / REL_MAX_EXACT) / math.log(REL_MAX_DIST / REL_MAX_EXACT)
                             * (REL_BUCKETS - REL_MAX_EXACT)).astype(jnp.int32)
    large = jnp.minimum(large, REL_BUCKETS - 1)
    return jnp.where(n < REL_MAX_EXACT, n, large)


def split_heads(t, n):
    B, S, _ = t.shape
    return t.reshape(B, S, n, HEAD_DIM).transpose(0, 2, 1, 3)


def moba_attention(q, k, v, bias_tab):
    B, H, S, D = q.shape
    SP = -(-S // MOBA_BLOCK) * MOBA_BLOCK
    pad = ((0, 0), (0, 0), (0, SP - S), (0, 0))
    q, k, v = jnp.pad(q, pad), jnp.pad(k, pad), jnp.pad(v, pad)
    NB = SP // MOBA_BLOCK
    kb = k.reshape(B, H, NB, MOBA_BLOCK, D)
    vb = v.reshape(B, H, NB, MOBA_BLOCK, D)
    kmean = jnp.mean(kb.astype(jnp.float32), axis=3)
    gate = jnp.einsum('bhsd,bhnd->bhsn', q.astype(jnp.float32), kmean)
    own = jnp.arange(SP) // MOBA_BLOCK
    past = jnp.arange(NB)[None, :] < own[:, None]
    gate = jnp.where(past, gate, NEG_INF)
    topk = min(MOBA_TOPK, NB)
    _, sel = lax.top_k(gate, topk)
    sel_ok = sel < own[None, None, :, None]
    nq = SP // MOBA_QCHUNK

    def to_chunks(t):
        return jnp.moveaxis(t.reshape(B, H, nq, MOBA_QCHUNK, t.shape[-1]), 2, 0)

    bidx = jnp.arange(B).reshape(B, 1, 1, 1)
    hidx = jnp.arange(H).reshape(1, H, 1, 1)
    h5 = jnp.arange(H).reshape(1, H, 1, 1, 1)
    scale = D ** -0.5
    nsel = topk * MOBA_BLOCK

    def chunk(args):
        ci, qi, si, oki = args
        q0 = ci * MOBA_QCHUNK
        qpos = q0 + jnp.arange(MOBA_QCHUNK)
        blk = q0 // MOBA_BLOCK
        kg = kb[bidx, hidx, si]
        vg = vb[bidx, hidx, si]
        kpos = si[..., None] * MOBA_BLOCK + jnp.arange(MOBA_BLOCK)
        dist_p = qpos[:, None, None] - kpos
        s_past = jnp.einsum('bhqd,bhqnjd->bhqnj', qi, kg).astype(jnp.float32) * scale
        s_past = s_past + bias_tab[t5_bucket(dist_p), h5]
        s_past = jnp.where(oki[..., None], s_past, NEG_INF).reshape(B, H, MOBA_QCHUNK, nsel)
        k_own = lax.dynamic_slice_in_dim(k, blk * MOBA_BLOCK, MOBA_BLOCK, axis=2)
        v_own = lax.dynamic_slice_in_dim(v, blk * MOBA_BLOCK, MOBA_BLOCK, axis=2)
        dist_o = qpos[:, None] - (blk * MOBA_BLOCK + jnp.arange(MOBA_BLOCK))[None, :]
        s_own = jnp.einsum('bhqd,bhkd->bhqk', qi, k_own).astype(jnp.float32) * scale
        s_own = jnp.where(dist_o >= 0, s_own + bias_tab.T[:, t5_bucket(dist_o)], NEG_INF)
        p = jax.nn.softmax(jnp.concatenate([s_past, s_own], axis=-1), axis=-1)
        p_past = p[..., :nsel].reshape(B, H, MOBA_QCHUNK, topk, MOBA_BLOCK).astype(v.dtype)
        p_own = p[..., nsel:].astype(v.dtype)
        return (jnp.einsum('bhqnj,bhqnjd->bhqd', p_past, vg)
                + jnp.einsum('bhqk,bhkd->bhqd', p_own, v_own))

    out = lax.map(chunk, (jnp.arange(nq), to_chunks(q), to_chunks(sel), to_chunks(sel_ok)))
    out = jnp.moveaxis(out, 0, 2).reshape(B, H, SP, D)
    return out[:, :, :S]


def diff_attention(q, k, v, lam_params, sub_gain, bias_tab, lam_init):
    B, H, S, D = v.shape
    lp = lam_params.astype(jnp.float32)
    lam = jnp.exp(jnp.sum(lp[0] * lp[1])) - jnp.exp(jnp.sum(lp[2] * lp[3])) + lam_init
    q1, q2 = q[..., :DIFF_QK_DIM], q[..., DIFF_QK_DIM:]
    k1, k2 = k[..., :DIFF_QK_DIM], k[..., DIFF_QK_DIM:]
    scale = DIFF_QK_DIM ** -0.5
    nb = S // DIFF_QBLOCK
    qb1 = jnp.moveaxis(q1.reshape(B, H, nb, DIFF_QBLOCK, DIFF_QK_DIM), 2, 0)
    qb2 = jnp.moveaxis(q2.reshape(B, H, nb, DIFF_QBLOCK, DIFF_QK_DIM), 2, 0)
    kpos = jnp.arange(S)

    def blk(args):
        i, a1, a2 = args
        qpos = i * DIFF_QBLOCK + jnp.arange(DIFF_QBLOCK)
        dist = qpos[:, None] - kpos[None, :]
        bias = bias_tab.T[:, t5_bucket(dist)]
        mask = dist >= 0
        s1 = jnp.einsum('bhqd,bhkd->bhqk', a1, k1).astype(jnp.float32) * scale + bias
        s2 = jnp.einsum('bhqd,bhkd->bhqk', a2, k2).astype(jnp.float32) * scale + bias
        p = (jax.nn.softmax(jnp.where(mask, s1, NEG_INF), axis=-1)
             - lam * jax.nn.softmax(jnp.where(mask, s2, NEG_INF), axis=-1))
        return jnp.einsum('bhqk,bhkd->bhqd', p.astype(v.dtype), v)

    o = lax.map(blk, (jnp.arange(nb), qb1, qb2))
    o = jnp.moveaxis(o, 0, 2).reshape(B, H, S, D)
    return rms_norm(o, sub_gain, DIFF_EPS) * (1.0 - lam_init)


def nsa_attention(q, kc, vc, ks, vs, kw, vw, gates, cmp_pos, cmp_w, bias_tab):
    B, Hn, S, D = q.shape
    G, R = NSA_KV_HEADS, NSA_GROUP
    qg = q.reshape(B, G, R, S, D)
    scale = D ** -0.5
    pos = jnp.arange(S)

    n_cmp = (S - NSA_CMP_LEN) // NSA_CMP_STRIDE + 1
    cmp_start = jnp.arange(n_cmp) * NSA_CMP_STRIDE
    cidx = cmp_start[:, None] + jnp.arange(NSA_CMP_LEN)[None, :]
    k_cmp = jnp.einsum('bgnld,lde->bgne', kc[:, :, cidx] + cmp_pos[0], cmp_w[0])
    v_cmp = jnp.einsum('bgnld,lde->bgne', vc[:, :, cidx] + cmp_pos[1], cmp_w[1])
    cmp_end = cmp_start + NSA_CMP_LEN - 1
    c_ok = cmp_end[None, :] <= pos[:, None]
    s_c = jnp.einsum('bgrsd,bgnd->bgrsn', qg, k_cmp).astype(jnp.float32) * scale
    p_c = jnp.where(c_ok, jax.nn.softmax(jnp.where(c_ok, s_c, NEG_INF), axis=-1), 0.0)
    o_cmp = jnp.einsum('bgrsn,bgnd->bgrsd', p_c.astype(vc.dtype), v_cmp)

    n_sel = S // NSA_SEL_BLOCK
    sel_start = jnp.arange(n_sel) * NSA_SEL_BLOCK
    overlap = ((cmp_start[:, None] <= (sel_start + NSA_SEL_BLOCK - 1)[None, :])
               & (cmp_end[:, None] >= sel_start[None, :])).astype(jnp.float32)
    imp = jnp.einsum('bgrsn,nm->bgsm', p_c, overlap)
    own = pos // NSA_SEL_BLOCK
    jj = jnp.arange(n_sel)[None, :]
    forced = (jj == 0) | (jj == own[:, None]) | (jj == own[:, None] - 1)
    imp = jnp.where(jj <= own[:, None], jnp.where(forced, NSA_FORCE, imp), NEG_INF)
    ksel = min(NSA_SEL_TOPK, n_sel)
    _, sidx = lax.top_k(imp, ksel)
    sok = sidx <= own[None, None, :, None]
    ksb = ks.reshape(B, G, n_sel, NSA_SEL_BLOCK, D)
    vsb = vs.reshape(B, G, n_sel, NSA_SEL_BLOCK, D)
    nq = S // NSA_SEL_QCHUNK
    q_ch = jnp.moveaxis(qg.reshape(B, G, R, nq, NSA_SEL_QCHUNK, D), 3, 0)
    s_ch = jnp.moveaxis(sidx.reshape(B, G, nq, NSA_SEL_QCHUNK, ksel), 2, 0)
    ok_ch = jnp.moveaxis(sok.reshape(B, G, nq, NSA_SEL_QCHUNK, ksel), 2, 0)
    tab_gr = bias_tab.reshape(REL_BUCKETS, G, R)
    bidx = jnp.arange(B).reshape(B, 1, 1, 1)
    gidx = jnp.arange(G).reshape(1, G, 1, 1)
    g6 = jnp.arange(G).reshape(1, G, 1, 1, 1, 1)
    r6 = jnp.arange(R).reshape(1, 1, R, 1, 1, 1)

    def sel_chunk(args):
        ci, qi, si, oki = args
        qpos = ci * NSA_SEL_QCHUNK + jnp.arange(NSA_SEL_QCHUNK)
        kg = ksb[bidx, gidx, si]
        vg = vsb[bidx, gidx, si]
        dist = qpos[:, None, None] - (si[..., None] * NSA_SEL_BLOCK + jnp.arange(NSA_SEL_BLOCK))
        bias = tab_gr[t5_bucket(dist)[:, :, None], g6, r6]
        mask = (oki[..., None] & (dist >= 0))[:, :, None]
        s = jnp.einsum('bgrqd,bgqnjd->bgrqnj', qi, kg).astype(jnp.float32) * scale
        s = jnp.where(mask, s + bias, NEG_INF)
        p = jax.nn.softmax(s.reshape(B, G, R, NSA_SEL_QCHUNK, -1), axis=-1).reshape(s.shape)
        return jnp.einsum('bgrqnj,bgqnjd->bgrqd', p.astype(vg.dtype), vg)

    o_sel = lax.map(sel_chunk, (jnp.arange(nq), q_ch, s_ch, ok_ch))
    o_sel = jnp.moveaxis(o_sel, 0, 3).reshape(B, G, R, S, D)

    band = NSA_WINDOW + NSA_WIN_QBLOCK
    kpad = jnp.pad(kw, ((0, 0), (0, 0), (NSA_WINDOW, 0), (0, 0)))
    vpad = jnp.pad(vw, ((0, 0), (0, 0), (NSA_WINDOW, 0), (0, 0)))
    nwb = S // NSA_WIN_QBLOCK
    qw = jnp.moveaxis(qg.reshape(B, G, R, nwb, NSA_WIN_QBLOCK, D), 3, 0)
    dist_w = jnp.arange(NSA_WIN_QBLOCK)[:, None] - jnp.arange(band)[None, :] + NSA_WINDOW
    bias_w = bias_tab.T[:, t5_bucket(dist_w)].reshape(G, R, NSA_WIN_QBLOCK, band)
    band_ok = (dist_w >= 0) & (dist_w < NSA_WINDOW)

    def win_block(args):
        i, qi = args
        kb_ = lax.dynamic_slice_in_dim(kpad, i * NSA_WIN_QBLOCK, band, axis=2)
        vb_ = lax.dynamic_slice_in_dim(vpad, i * NSA_WIN_QBLOCK, band, axis=2)
        kpos = i * NSA_WIN_QBLOCK - NSA_WINDOW + jnp.arange(band)
        mask = band_ok & (kpos >= 0)[None, :]
        s = jnp.einsum('bgrqd,bgkd->bgrqk', qi, kb_).astype(jnp.float32) * scale
        p = jax.nn.softmax(jnp.where(mask, s + bias_w, NEG_INF), axis=-1)
        return jnp.einsum('bgrqk,bgkd->bgrqd', p.astype(vb_.dtype), vb_)

    o_win = lax.map(win_block, (jnp.arange(nwb), qw))
    o_win = jnp.moveaxis(o_win, 0, 3).reshape(B, G, R, S, D)

    g = jax.nn.sigmoid(gates.astype(jnp.float32)).transpose(0, 2, 1, 3).reshape(B, G, R, S, NSA_BRANCHES)
    g = g.astype(q.dtype)
    o = g[..., 0:1] * o_cmp + g[..., 1:2] * o_sel + g[..., 2:3] * o_win
    return o.reshape(B, Hn, S, D)


def peer_layer(xn, w_q, sub_keys, u_tab, v_tab):
    B, S, D = xn.shape
    N = B * S
    xf = xn.reshape(N, D)
    qry = jnp.einsum('nd,de->ne', xf, w_q).reshape(N, PEER_HEADS, 2, PEER_KEY_HALF)
    sc = jnp.einsum('nhpd,hpkd->nhpk', qry, sub_keys).astype(jnp.float32)
    s1, i1 = lax.top_k(sc[:, :, 0], PEER_TOPK)
    s2, i2 = lax.top_k(sc[:, :, 1], PEER_TOPK)
    cand = (s1[..., :, None] + s2[..., None, :]).reshape(N, PEER_HEADS, PEER_TOPK * PEER_TOPK)
    cidx = (i1[..., :, None] * PEER_NKEYS + i2[..., None, :]).reshape(N, PEER_HEADS, PEER_TOPK * PEER_TOPK)
    top, tpos = lax.top_k(cand, PEER_TOPK)
    eidx = jnp.take_along_axis(cidx, tpos, axis=-1)
    gate = jax.nn.softmax(top, axis=-1)
    nc = N // PEER_TCHUNK

    def tok_chunk(args):
        xc, ec, gc = args
        ug = u_tab[ec]
        h = jnp.einsum('cd,chkd->chk', xc, ug).astype(jnp.float32)
        a = (jax.nn.gelu(h, approximate=False) * gc).astype(xc.dtype)
        vg = v_tab[ec]
        return jnp.einsum('chk,chkd->cd', a, vg)

    out = lax.map(tok_chunk, (xf.reshape(nc, PEER_TCHUNK, D),
                              eidx.reshape(nc, PEER_TCHUNK, PEER_HEADS, PEER_TOPK),
                              gate.reshape(nc, PEER_TCHUNK, PEER_HEADS, PEER_TOPK)))
    return out.reshape(B, S, D)


def setup_inputs(seed: int = 0) -> dict:
    key = jax.random.key(seed)
    ks = jax.random.split(key, 15)
    nrm = jax.random.normal
    f32 = jnp.float32
    return {
        'x': nrm(ks[0], (BATCH, SEQ, D_MODEL), f32),
        'rel_bias': 0.5 * nrm(ks[1], (REL_BUCKETS, N_HEADS), f32),
        'attn_norm': 1.0 + 0.05 * nrm(ks[2], (DEPTH, D_MODEL), f32),
        'w_in': nrm(ks[3], (DEPTH, D_MODEL, IN_WIDTH), f32) * D_MODEL ** -0.5,
        'diff_lambda': 0.1 * nrm(ks[4], (DEPTH, 4, DIFF_QK_DIM), f32),
        'diff_subln': 1.0 + 0.05 * nrm(ks[5], (DEPTH, HEAD_DIM), f32),
        'nsa_cmp_pos': 0.1 * nrm(ks[6], (DEPTH, 2, NSA_CMP_LEN, HEAD_DIM), f32),
        'nsa_cmp_w': nrm(ks[7], (DEPTH, 2, NSA_CMP_LEN, HEAD_DIM, HEAD_DIM), f32) * (NSA_CMP_LEN * HEAD_DIM) ** -0.5,
        'w_out': nrm(ks[8], (DEPTH, MIX_WIDTH, D_MODEL), f32) * MIX_WIDTH ** -0.5,
        'ffn_norm': 1.0 + 0.05 * nrm(ks[9], (DEPTH, D_MODEL), f32),
        'peer_wq': nrm(ks[10], (DEPTH, D_MODEL, PEER_HEADS * 2 * PEER_KEY_HALF), f32) * D_MODEL ** -0.5,
        'peer_subkeys': nrm(ks[11], (DEPTH, PEER_HEADS, 2, PEER_NKEYS, PEER_KEY_HALF), f32) * PEER_KEY_HALF ** -0.5,
        'peer_u': nrm(ks[12], (DEPTH, PEER_EXPERTS, D_MODEL), f32) * D_MODEL ** -0.5,
        'peer_v': nrm(ks[13], (DEPTH, PEER_EXPERTS, D_MODEL), f32) * PEER_HEADS ** -0.5,
        'final_norm': 1.0 + 0.05 * nrm(ks[14], (D_MODEL,), f32),
    }


def reference(x, rel_bias, attn_norm, w_in, diff_lambda, diff_subln, nsa_cmp_pos, nsa_cmp_w,
              w_out, ffn_norm, peer_wq, peer_subkeys, peer_u, peer_v, final_norm):
    B, S, _ = x.shape
    offs = [int(o) for o in np.cumsum(IN_SPLITS)[:-1]]
    bias_moba = rel_bias[:, :MOBA_HEADS]
    bias_diff = rel_bias[:, MOBA_HEADS:MOBA_HEADS + DIFF_HEADS]
    bias_nsa = rel_bias[:, MOBA_HEADS + DIFF_HEADS:]
    for l in range(DEPTH):
        h = rms_norm(x, attn_norm[l])
        proj = jnp.einsum('bsd,de->bse', h, w_in[l])
        (mq, mk, mv, dq, dk, dv, nq, nkc, nvc, nks, nvs, nkw, nvw, ngate) = jnp.split(proj, offs, axis=-1)
        o_moba = moba_attention(split_heads(mq, MOBA_HEADS), split_heads(mk, MOBA_HEADS),
                                split_heads(mv, MOBA_HEADS), bias_moba)
        lam_init = 0.8 - 0.6 * math.exp(-0.3 * l)
        o_diff = diff_attention(split_heads(dq, DIFF_HEADS), split_heads(dk, DIFF_HEADS),
                                split_heads(dv, DIFF_HEADS), diff_lambda[l], diff_subln[l],
                                bias_diff, lam_init)
        o_nsa = nsa_attention(split_heads(nq, NSA_HEADS),
                              split_heads(nkc, NSA_KV_HEADS), split_heads(nvc, NSA_KV_HEADS),
                              split_heads(nks, NSA_KV_HEADS), split_heads(nvs, NSA_KV_HEADS),
                              split_heads(nkw, NSA_KV_HEADS), split_heads(nvw, NSA_KV_HEADS),
                              ngate.reshape(B, S, NSA_HEADS, NSA_BRANCHES),
                              nsa_cmp_pos[l], nsa_cmp_w[l], bias_nsa)
        o = jnp.concatenate([o_moba, o_diff, o_nsa], axis=1)
        o = o.transpose(0, 2, 1, 3).reshape(B, S, MIX_WIDTH)
        x = x + jnp.einsum('bse,ed->bsd', o, w_out[l])
        h2 = rms_norm(x, ffn_norm[l])
        x = x + peer_layer(h2, peer_wq[l], peer_subkeys[l], peer_u[l], peer_v[l])
    return rms_norm(x, final_norm)
```

```python
import functools
import math

import jax
import jax.numpy as jnp
from jax import lax
from jax.experimental import pallas as pl
from jax.experimental.pallas import tpu as pltpu

F32 = jnp.float32
BF16 = jnp.bfloat16
NEG = -1e30

HEAD_DIM = 128
LANES = 128
VMEM_LIMIT = 56 * 1024 * 1024

N_MOBA = 4
N_DIFF = 4
N_NSA = 8
NSA_G = 2
NSA_R = 4
MOBA_BLOCK = 256
MOBA_TOPK = 3
DIFF_EPS = 1e-5
NORM_EPS = 1e-6
CMP_LEN = 32
CMP_STRIDE = 16
SEL_BLOCK = 64
SEL_TOPK = 16
NSA_WINDOW = 512
NSA_FORCE = 1e6
REL_BUCKETS = 32
REL_MAX_EXACT = 16
REL_MAX_DIST = 128
PEER_HEADS = 8
PEER_NKEYS = 128
PEER_TOPK = 16

_MQ, _MK, _MV = 0, 4, 8
_DQ, _DK, _DV = 12, 16, 20
_NQ = 24
_NKC, _NVC, _NKS, _NVS, _NKW, _NVW = 32, 34, 36, 38, 40, 42
_N_PROJ_HEADS = 44

_NT = (((1,), (1,)), ((), ()))


def _cparams(*sem):
    return pltpu.CompilerParams(dimension_semantics=sem, vmem_limit_bytes=VMEM_LIMIT)


def _dot(a, b):
    return jnp.dot(a, b, preferred_element_type=F32)


def _dot_nt(a, b):
    return lax.dot_general(a, b, _NT, preferred_element_type=F32)


def _t5_bucket(dist):
    n = jnp.maximum(dist, 0)
    nf = jnp.maximum(n, REL_MAX_EXACT).astype(F32)
    large = REL_MAX_EXACT + (jnp.log(nf / REL_MAX_EXACT) / math.log(REL_MAX_DIST / REL_MAX_EXACT)
                             * (REL_BUCKETS - REL_MAX_EXACT)).astype(jnp.int32)
    large = jnp.minimum(large, REL_BUCKETS - 1)
    return jnp.where(n < REL_MAX_EXACT, n, large)


def _bias_tiles(tab, tq, tk, offsets, lo=0, hi=None):
    t = jnp.arange(tq)[:, None]
    c = jnp.arange(tk)[None, :]
    tiles = []
    for off in offsets:
        dist = off + t - c
        b = tab.T[:, _t5_bucket(dist)]
        ok = dist >= lo
        if hi is not None:
            ok = ok & (dist < hi)
        tiles.append(jnp.where(ok[None], b, NEG))
    return jnp.stack(tiles, axis=1).astype(F32)


def _inproj_kernel(x_ref, g_ref, w_ref, wg_ref, o_ref, go_ref, xn_ref, *, heads_per_tile):
    @pl.when(pl.program_id(1) == 0)
    def _():
        x = x_ref[...]
        y = x * lax.rsqrt(jnp.mean(x * x, axis=-1, keepdims=True) + NORM_EPS)
        xn_ref[...] = (y * g_ref[...]).astype(BF16)
        go_ref[...] = _dot(xn_ref[...], wg_ref[...])

    acc = _dot(xn_ref[...], w_ref[...])
    for hh in range(heads_per_tile):
        o_ref[0, hh] = acc[:, hh * HEAD_DIM:(hh + 1) * HEAD_DIM].astype(BF16)


def _inproj(x2, gain, w, wg, batch, seq):
    n, d = x2.shape
    e = w.shape[1]
    tm, tn = 512, 512
    hpt = tn // HEAD_DIM
    spt = seq // tm
    return pl.pallas_call(
        functools.partial(_inproj_kernel, heads_per_tile=hpt),
        grid=(n // tm, e // tn),
        in_specs=[
            pl.BlockSpec((tm, d), lambda i, j: (i, 0)),
            pl.BlockSpec((1, d), lambda i, j: (0, 0)),
            pl.BlockSpec((d, tn), lambda i, j: (0, j)),
            pl.BlockSpec((d, LANES), lambda i, j: (0, 0)),
        ],
        out_specs=[
            pl.BlockSpec((1, hpt, tm, HEAD_DIM), lambda i, j: (i // spt, j, i % spt, 0)),
            pl.BlockSpec((tm, LANES), lambda i, j: (i, 0)),
        ],
        out_shape=[
            jax.ShapeDtypeStruct((batch, e // HEAD_DIM, seq, HEAD_DIM), BF16),
            jax.ShapeDtypeStruct((n, LANES), F32),
        ],
        scratch_shapes=[pltpu.VMEM((tm, d), BF16)],
        compiler_params=_cparams("parallel", "arbitrary"),
        name="inproj",
    )(x2, gain, w, wg)


def _flash_step(carry, q, kt, vt, extra, scale):
    m, l, acc = carry
    s = _dot_nt(q, kt) * scale + extra
    m_new = jnp.maximum(m, jnp.max(s, axis=1, keepdims=True))
    alpha = jnp.exp(m - m_new)
    p = jnp.exp(s - m_new)
    l = alpha * l + jnp.sum(p, axis=1, keepdims=True)
    acc = alpha * acc + _dot(p.astype(BF16), vt)
    return m_new, l, acc


def _flash_init(rows):
    return (jnp.full((rows, 1), NEG, F32), jnp.zeros((rows, 1), F32), jnp.zeros((rows, HEAD_DIM), F32))


def _rank_before(vals, ncols):
    col = lax.broadcasted_iota(jnp.int32, vals.shape, 1)
    rank = jnp.zeros(vals.shape, F32)
    for c in range(ncols):
        vc = vals[:, c:c + 1]
        tie = jnp.where(col > c, 1.0, 0.0)
        rank = rank + jnp.where(vc > vals, 1.0, jnp.where(vc == vals, tie, 0.0))
    return rank


def _moba_kernel(q_ref, k_ref, v_ref, bt_ref, o_ref, kmean_ref, *, scale, nblk):
    i = pl.program_id(2)
    tq = q_ref.shape[2]

    @pl.when(i == 0)
    def _():
        kmean_ref[...] = jnp.zeros_like(kmean_ref)
        kf = k_ref[0, 0].astype(F32).reshape(nblk, tq, HEAD_DIM)
        kmean_ref[0:nblk, :] = jnp.mean(kf, axis=1)

    q = q_ref[0, 0]
    col = lax.broadcasted_iota(jnp.int32, (tq, LANES), 1)
    gate = _dot_nt(q, kmean_ref[...].astype(BF16))
    gate = jnp.where(col < i, gate, NEG)
    rank = _rank_before(gate, nblk)
    selneg = jnp.where((rank < MOBA_TOPK) & (col < i), 0.0, NEG)

    carry = _flash_step(_flash_init(tq), q, k_ref[0, 0, pl.ds(pl.multiple_of(i * tq, tq), tq), :],
                        v_ref[0, 0, pl.ds(pl.multiple_of(i * tq, tq), tq), :], bt_ref[0, 0], scale)

    def body(j, carry):
        start = pl.multiple_of(j * tq, tq)
        selcol = jnp.sum(jnp.where(col == j, selneg, 0.0), axis=1, keepdims=True)
        extra = bt_ref[0, jnp.minimum(i - j, 2)] + selcol
        return _flash_step(carry, q, k_ref[0, 0, pl.ds(start, tq), :], v_ref[0, 0, pl.ds(start, tq), :],
                           extra, scale)

    _, l, acc = lax.fori_loop(0, i, body, carry)
    o_ref[0, 0] = (acc / l).astype(o_ref.dtype)


def _moba(proj, bias_tiles):
    b, _, s, _ = proj.shape
    tq = MOBA_BLOCK
    nblk = s // tq
    return pl.pallas_call(
        functools.partial(_moba_kernel, scale=HEAD_DIM ** -0.5, nblk=nblk),
        grid=(b, N_MOBA, nblk),
        in_specs=[
            pl.BlockSpec((1, 1, tq, HEAD_DIM), lambda bb, h, i: (bb, _MQ + h, i, 0)),
            pl.BlockSpec((1, 1, s, HEAD_DIM), lambda bb, h, i: (bb, _MK + h, 0, 0)),
            pl.BlockSpec((1, 1, s, HEAD_DIM), lambda bb, h, i: (bb, _MV + h, 0, 0)),
            pl.BlockSpec((1, 3, tq, tq), lambda bb, h, i: (h, 0, 0, 0)),
        ],
        out_specs=pl.BlockSpec((1, 1, tq, HEAD_DIM), lambda bb, h, i: (bb, h, i, 0)),
        out_shape=jax.ShapeDtypeStruct((b, N_MOBA, s, HEAD_DIM), BF16),
        scratch_shapes=[pltpu.VMEM((LANES, HEAD_DIM), F32)],
        compiler_params=_cparams("parallel", "parallel", "arbitrary"),
        name="moba",
    )(proj, proj, proj, bias_tiles)


def _diff_kernel(q_ref, k_ref, v_ref, bt_ref, lam_ref, gain_ref, o_ref, *, scale, lam_init):
    i = pl.program_id(2)
    tq = q_ref.shape[2]
    q = q_ref[0, 0]
    lane = lax.broadcasted_iota(jnp.int32, q.shape, 1)
    half = HEAD_DIM // 2
    q1 = jnp.where(lane < half, q, jnp.zeros_like(q))
    q2 = jnp.where(lane >= half, q, jnp.zeros_like(q))

    def body(j, carry):
        c1, c2 = carry
        start = pl.multiple_of(j * tq, tq)
        kt = k_ref[0, 0, pl.ds(start, tq), :]
        vt = v_ref[0, 0, pl.ds(start, tq), :]
        extra = bt_ref[0, jnp.minimum(i - j, 2)]
        return _flash_step(c1, q1, kt, vt, extra, scale), _flash_step(c2, q2, kt, vt, extra, scale)

    (_, l1, a1), (_, l2, a2) = lax.fori_loop(0, i + 1, body, (_flash_init(tq), _flash_init(tq)))

    lp = lam_ref[...]
    lam = (jnp.exp(jnp.sum(lp[0:1] * lp[1:2], axis=1, keepdims=True))
           - jnp.exp(jnp.sum(lp[2:3] * lp[3:4], axis=1, keepdims=True)) + lam_init)
    o = a1 / l1 - lam * (a2 / l2)
    y = o * lax.rsqrt(jnp.mean(o * o, axis=-1, keepdims=True) + DIFF_EPS)
    o_ref[0, 0] = ((y * gain_ref[...]) * (1.0 - lam_init)).astype(o_ref.dtype)


def _diff(proj, bias_tiles, lam_params, sub_gain, lam_init):
    b, _, s, _ = proj.shape
    tq = 256
    return pl.pallas_call(
        functools.partial(_diff_kernel, scale=(HEAD_DIM // 2) ** -0.5, lam_init=lam_init),
        grid=(b, N_DIFF, s // tq),
        in_specs=[
            pl.BlockSpec((1, 1, tq, HEAD_DIM), lambda bb, h, i: (bb, _DQ + h, i, 0)),
            pl.BlockSpec((1, 1, s, HEAD_DIM), lambda bb, h, i: (bb, _DK + h, 0, 0)),
            pl.BlockSpec((1, 1, s, HEAD_DIM), lambda bb, h, i: (bb, _DV + h, 0, 0)),
            pl.BlockSpec((1, 3, tq, tq), lambda bb, h, i: (h, 0, 0, 0)),
            pl.BlockSpec(lam_params.shape, lambda bb, h, i: (0, 0)),
            pl.BlockSpec((1, HEAD_DIM), lambda bb, h, i: (0, 0)),
        ],
        out_specs=pl.BlockSpec((1, 1, tq, HEAD_DIM), lambda bb, h, i: (bb, h, i, 0)),
        out_shape=jax.ShapeDtypeStruct((b, N_DIFF, s, HEAD_DIM), BF16),
        compiler_params=_cparams("parallel", "parallel", "arbitrary"),
        name="diff",
    )(proj, proj, proj, bias_tiles, lam_params, sub_gain)


def _nsa_compress_kernel(x_ref, w_ref, pos_ref, o_ref):
    x = x_ref[0, 0]
    w0, w1 = w_ref[0, 0], w_ref[0, 1]
    y0 = _dot(x, w0)
    y1 = _dot(x, w1)
    pv = _dot(pos_ref[0], w0)[0:1] + _dot(pos_ref[0], w1)[1:2]
    n = y1.shape[0]
    y1_next = pltpu.roll(y1, n - 1, 0)
    o_ref[0, 0] = (y0 + y1_next + pv).astype(o_ref.dtype)


def _nsa_compress(xkv, w, pos):
    b, hh, nrow, wid = xkv.shape
    return pl.pallas_call(
        _nsa_compress_kernel,
        grid=(b, hh),
        in_specs=[
            pl.BlockSpec((1, 1, nrow, wid), lambda bb, h: (bb, h, 0, 0)),
            pl.BlockSpec((1, 2, wid, HEAD_DIM), lambda bb, h: (h // NSA_G, 0, 0, 0)),
            pl.BlockSpec((1, 8, wid), lambda bb, h: (h // NSA_G, 0, 0)),
        ],
        out_specs=pl.BlockSpec((1, 1, nrow, HEAD_DIM), lambda bb, h: (bb, h, 0, 0)),
        out_shape=jax.ShapeDtypeStruct((b, hh, nrow, HEAD_DIM), BF16),
        compiler_params=_cparams("parallel", "parallel"),
        name="nsa_compress",
    )(xkv, w, pos)


def _nsa_cmp_kernel(q_ref, kc_ref, vc_ref, ov_ref, o_ref, sel_ref, *, scale, nsel):
    i = pl.program_id(2)
    tq = q_ref.shape[2]
    ncmp = kc_ref.shape[2]
    kc = kc_ref[0, 0]
    vc = vc_ref[0, 0]
    ov = ov_ref[...]
    pos = i * tq + lax.broadcasted_iota(jnp.int32, (tq, ncmp), 0)
    ncol = lax.broadcasted_iota(jnp.int32, (tq, ncmp), 1)
    ok = (ncol * CMP_STRIDE + (CMP_LEN - 1)) <= pos
    imp = jnp.zeros((tq, LANES), F32)
    for r in range(NSA_R):
        s = jnp.where(ok, _dot_nt(q_ref[0, r], kc) * scale, NEG)
        m = jnp.max(s, axis=1, keepdims=True)
        p = jnp.where(ok, jnp.exp(s - m), 0.0)
        l = jnp.sum(p, axis=1, keepdims=True)
        pn = p / jnp.where(l > 0.0, l, 1.0)
        hi = pn.astype(BF16)
        lo = (pn - hi.astype(F32)).astype(BF16)
        o_ref[0, r] = _dot(hi, vc)
        imp = imp + _dot(hi, ov) + _dot(lo, ov)

    jj = lax.broadcasted_iota(jnp.int32, (tq, LANES), 1)
    own = (i * tq + lax.broadcasted_iota(jnp.int32, (tq, LANES), 0)) // SEL_BLOCK
    forced = (jj == 0) | (jj == own) | (jj == own - 1)
    visible = jj <= own
    imp = jnp.where(visible, jnp.where(forced, NSA_FORCE, imp), NEG)
    rank = _rank_before(imp, nsel)
    sel_ref[0, 0] = jnp.where((rank < SEL_TOPK) & visible, 0.0, NEG)


def _nsa_cmp(proj, kvc, overlap):
    b, _, s, _ = proj.shape
    tq = 128
    ncmp = kvc.shape[2]
    return pl.pallas_call(
        functools.partial(_nsa_cmp_kernel, scale=HEAD_DIM ** -0.5, nsel=s // SEL_BLOCK),
        grid=(b, NSA_G, s // tq),
        in_specs=[
            pl.BlockSpec((1, NSA_R, tq, HEAD_DIM), lambda bb, g, i: (bb, _NQ // NSA_R + g, i, 0)),
            pl.BlockSpec((1, 1, ncmp, HEAD_DIM), lambda bb, g, i: (bb, g, 0, 0)),
            pl.BlockSpec((1, 1, ncmp, HEAD_DIM), lambda bb, g, i: (bb, NSA_G + g, 0, 0)),
            pl.BlockSpec(overlap.shape, lambda bb, g, i: (0, 0)),
        ],
        out_specs=[
            pl.BlockSpec((1, NSA_R, tq, HEAD_DIM), lambda bb, g, i: (bb, g, i, 0)),
            pl.BlockSpec((1, 1, tq, LANES), lambda bb, g, i: (bb, g, i, 0)),
        ],
        out_shape=[
            jax.ShapeDtypeStruct((b, N_NSA, s, HEAD_DIM), F32),
            jax.ShapeDtypeStruct((b, NSA_G, s, LANES), F32),
        ],
        compiler_params=_cparams("parallel", "parallel", "parallel"),
        name="nsa_cmp",
    )(proj, kvc, kvc, overlap)


def _nsa_sel_kernel(q_ref, k_ref, v_ref, sel_ref, ex_ref, bt_ref, o_ref, *, scale):
    i = pl.program_id(2)
    tq = q_ref.shape[2]
    tk = ex_ref.shape[2]
    rows = NSA_R * tq
    q = q_ref[0].reshape(rows, HEAD_DIM)
    selneg = sel_ref[0, 0].astype(BF16)
    ratio = tk // tq

    def body(kt, carry):
        start = pl.multiple_of(kt * tk, tk)
        mask = _dot(selneg, ex_ref[kt])
        mask = jnp.concatenate([mask] * NSA_R, axis=0)
        extra = bt_ref[0, jnp.minimum(i - ratio * kt, 3)] + mask
        return _flash_step(carry, q, k_ref[0, 0, pl.ds(start, tk), :], v_ref[0, 0, pl.ds(start, tk), :],
                           extra, scale)

    _, l, acc = lax.fori_loop(0, i // ratio + 1, body, _flash_init(rows))
    o_ref[0] = (acc / l).reshape(NSA_R, tq, HEAD_DIM)


def _nsa_sel(proj, selneg, expand, bias_tiles):
    b, _, s, _ = proj.shape
    tq = 128
    tk = expand.shape[2]
    rows = NSA_R * tq
    return pl.pallas_call(
        functools.partial(_nsa_sel_kernel, scale=HEAD_DIM ** -0.5),
        grid=(b, NSA_G, s // tq),
        in_specs=[
            pl.BlockSpec((1, NSA_R, tq, HEAD_DIM), lambda bb, g, i: (bb, _NQ // NSA_R + g, i, 0)),
            pl.BlockSpec((1, 1, s, HEAD_DIM), lambda bb, g, i: (bb, _NKS + g, 0, 0)),
            pl.BlockSpec((1, 1, s, HEAD_DIM), lambda bb, g, i: (bb, _NVS + g, 0, 0)),
            pl.BlockSpec((1, 1, tq, LANES), lambda bb, g, i: (bb, g, i, 0)),
            pl.BlockSpec(expand.shape, lambda bb, g, i: (0, 0, 0)),
            pl.BlockSpec((1, 4, rows, tk), lambda bb, g, i: (g, 0, 0, 0)),
        ],
        out_specs=pl.BlockSpec((1, NSA_R, tq, HEAD_DIM), lambda bb, g, i: (bb, g, i, 0)),
        out_shape=jax.ShapeDtypeStruct((b, N_NSA, s, HEAD_DIM), F32),
        compiler_params=_cparams("parallel", "parallel", "arbitrary"),
        name="nsa_sel",
    )(proj, proj, proj, selneg, expand, bias_tiles)


def _nsa_win_kernel(q_ref, k_ref, v_ref, bt_ref, oc_ref, os_ref, gate_ref, o_ref, *, scale, nband):
    i = pl.program_id(2)
    tq = q_ref.shape[2]
    rows = NSA_R * tq
    q = q_ref[0].reshape(rows, HEAD_DIM)
    carry = _flash_init(rows)
    for m in range(nband):
        kt = i - m
        start = pl.multiple_of(jnp.maximum(kt, 0) * tq, tq)
        extra = bt_ref[0, m] + jnp.where(kt >= 0, 0.0, NEG)
        carry = _flash_step(carry, q, k_ref[0, 0, pl.ds(start, tq), :], v_ref[0, 0, pl.ds(start, tq), :],
                            extra, scale)
    _, l, acc = carry
    ow = (acc / l).reshape(NSA_R, tq, HEAD_DIM)
    g = jax.nn.sigmoid(gate_ref[0, 0])
    for r in range(NSA_R):
        gc, gs, gw = (g[:, 3 * r + br:3 * r + br + 1] for br in range(3))
        o_ref[0, r] = (gc * oc_ref[0, r] + gs * os_ref[0, r] + gw * ow[r]).astype(o_ref.dtype)


def _nsa_win_combine(proj, bias_tiles, o_cmp, o_sel, gates3):
    b, _, s, _ = proj.shape
    tq = 128
    rows = NSA_R * tq
    nband = bias_tiles.shape[1]
    blk = pl.BlockSpec((1, NSA_R, tq, HEAD_DIM), lambda bb, g, i: (bb, g, i, 0))
    return pl.pallas_call(
        functools.partial(_nsa_win_kernel, scale=HEAD_DIM ** -0.5, nband=nband),
        grid=(b, NSA_G, s // tq),
        in_specs=[
            pl.BlockSpec((1, NSA_R, tq, HEAD_DIM), lambda bb, g, i: (bb, _NQ // NSA_R + g, i, 0)),
            pl.BlockSpec((1, 1, s, HEAD_DIM), lambda bb, g, i: (bb, _NKW + g, 0, 0)),
            pl.BlockSpec((1, 1, s, HEAD_DIM), lambda bb, g, i: (bb, _NVW + g, 0, 0)),
            pl.BlockSpec((1, nband, rows, tq), lambda bb, g, i: (g, 0, 0, 0)),
            blk, blk,
            pl.BlockSpec((1, 1, tq, LANES), lambda bb, g, i: (bb, g, i, 0)),
        ],
        out_specs=blk,
        out_shape=jax.ShapeDtypeStruct((b, N_NSA, s, HEAD_DIM), BF16),
        compiler_params=_cparams("parallel", "parallel", "parallel"),
        name="nsa_win",
    )(proj, proj, proj, bias_tiles, o_cmp, o_sel, gates3)


def _outproj_kernel(om_ref, od_ref, on_ref, w_ref, x_ref, o_ref, cat_ref):
    h = 0
    for ref in (om_ref, od_ref, on_ref):
        for hh in range(ref.shape[1]):
            cat_ref[:, h * HEAD_DIM:(h + 1) * HEAD_DIM] = ref[0, hh]
            h += 1
    o_ref[...] = x_ref[...] + _dot(cat_ref[...], w_ref[...])


def _outproj(o_moba, o_diff, o_nsa, w, x2):
    n, d = x2.shape
    b, _, s, _ = o_moba.shape
    tm = 512
    spt = s // tm

    def hspec(nh):
        return pl.BlockSpec((1, nh, tm, HEAD_DIM), lambda i: (i // spt, 0, i % spt, 0))

    return pl.pallas_call(
        _outproj_kernel,
        grid=(n // tm,),
        in_specs=[hspec(N_MOBA), hspec(N_DIFF), hspec(N_NSA),
                  pl.BlockSpec(w.shape, lambda i: (0, 0)),
                  pl.BlockSpec((tm, d), lambda i: (i, 0))],
        out_specs=pl.BlockSpec((tm, d), lambda i: (i, 0)),
        out_shape=jax.ShapeDtypeStruct((n, d), F32),
        scratch_shapes=[pltpu.VMEM((tm, w.shape[0]), BF16)],
        compiler_params=_cparams("parallel"),
        name="outproj",
    )(o_moba, o_diff, o_nsa, w, x2)


def _peer_query_kernel(x_ref, g_ref, wq_ref, sk_ref, sc_ref, xt_ref):
    x = x_ref[...]
    y = x * lax.rsqrt(jnp.mean(x * x, axis=-1, keepdims=True) + NORM_EPS)
    xn = y * g_ref[...]
    xt_ref[...] = xn.T.astype(BF16)
    qry = _dot(xn.astype(BF16), wq_ref[...]).astype(BF16)
    for hp in range(sk_ref.shape[0]):
        sc_ref[hp] = _dot_nt(sk_ref[hp], qry[:, hp * PEER_NKEYS:(hp + 1) * PEER_NKEYS])


def _peer_query(x2, gain, wq, subkeys):
    n, d = x2.shape
    tm = 512
    nhp = subkeys.shape[0]
    return pl.pallas_call(
        _peer_query_kernel,
        grid=(n // tm,),
        in_specs=[
            pl.BlockSpec((tm, d), lambda i: (i, 0)),
            pl.BlockSpec((1, d), lambda i: (0, 0)),
            pl.BlockSpec(wq.shape, lambda i: (0, 0)),
            pl.BlockSpec(subkeys.shape, lambda i: (0, 0, 0)),
        ],
        out_specs=[
            pl.BlockSpec((nhp, PEER_NKEYS, tm), lambda i: (0, 0, i)),
            pl.BlockSpec((d, tm), lambda i: (0, i)),
        ],
        out_shape=[
            jax.ShapeDtypeStruct((nhp, PEER_NKEYS, n), F32),
            jax.ShapeDtypeStruct((d, n), BF16),
        ],
        compiler_params=_cparams("parallel"),
        name="peer_query",
    )(x2, gain, wq, subkeys)


def _top_sorted(a, k):
    rowid = lax.broadcasted_iota(jnp.int32, a.shape, 0)
    nrow = a.shape[0]
    member = jnp.zeros(a.shape, F32)
    vals = []
    for _ in range(k):
        m = jnp.max(a, axis=0, keepdims=True)
        first = jnp.min(jnp.where(a == m, rowid, nrow), axis=0, keepdims=True)
        hit = rowid == first
        member = jnp.where(hit, 1.0, member)
        a = jnp.where(hit, NEG, a)
        vals.append(m)
    return vals, member


def _peer_route_kernel(sc_ref, w1_ref, w2_ref, th_ref):
    k = PEER_TOPK
    for h in range(PEER_HEADS):
        a = sc_ref[2 * h]
        b = sc_ref[2 * h + 1]
        av, amem = _top_sorted(a, k)
        bv, bmem = _top_sorted(b, k)
        pairs = [(r, c) for r in range(k) for c in range(k) if (r + 1) * (c + 1) <= k + 1]
        nrow = -(-len(pairs) // 8) * 8
        crow = lax.broadcasted_iota(jnp.int32, (nrow, a.shape[1]), 0)
        cmat = jnp.full((nrow, a.shape[1]), NEG, F32)
        for idx, (r, c) in enumerate(pairs):
            cmat = jnp.where(crow == idx, av[r] + bv[c], cmat)
        top, _ = _top_sorted(cmat, k + 1)
        peak = top[0]
        z = jnp.zeros_like(peak)
        for t in top[:k]:
            z = z + jnp.exp(t - peak)
        mid = 0.5 * (top[k - 1] + top[k])
        th_ref[h:h + 1, :] = jnp.exp(mid - peak) / z
        w1_ref[h] = jnp.where(amem > 0.0, jnp.exp(a - av[0]), 0.0)
        w2_ref[h] = jnp.where(bmem > 0.0, jnp.exp(b - bv[0]), 0.0) / z


def _peer_route(sc):
    nhp, nk, n = sc.shape
    t = 256
    return pl.pallas_call(
        _peer_route_kernel,
        grid=(n // t,),
        in_specs=[pl.BlockSpec((nhp, nk, t), lambda i: (0, 0, i))],
        out_specs=[
            pl.BlockSpec((PEER_HEADS, nk, t), lambda i: (0, 0, i)),
            pl.BlockSpec((PEER_HEADS, nk, t), lambda i: (0, 0, i)),
            pl.BlockSpec((PEER_HEADS, t), lambda i: (0, i)),
        ],
        out_shape=[
            jax.ShapeDtypeStruct((PEER_HEADS, nk, n), F32),
            jax.ShapeDtypeStruct((PEER_HEADS, nk, n), F32),
            jax.ShapeDtypeStruct((PEER_HEADS, n), F32),
        ],
        compiler_params=_cparams("parallel"),
        name="peer_route",
    )(sc)


def _peer_expert_kernel(xt_ref, u_ref, vt_ref, w1_ref, w2_ref, th_ref, o_ref, h_ref, a_ref, *, rows_per_tile):
    e = pl.program_id(1)

    @pl.when(e == 0)
    def _():
        o_ref[...] = jnp.zeros_like(o_ref)

    h_ref[...] = _dot(u_ref[...], xt_ref[...])

    def body(ii, _):
        start = pl.multiple_of(ii * PEER_NKEYS, PEER_NKEYS)
        g = None
        for hd in range(PEER_HEADS):
            p = w1_ref[hd, pl.ds(e * rows_per_tile + ii, 1), :] * w2_ref[hd]
            term = jnp.where(p >= th_ref[hd:hd + 1, :], p, 0.0)
            g = term if g is None else g + term
        hv = h_ref[pl.ds(start, PEER_NKEYS), :]
        gelu = 0.5 * hv * (1.0 + lax.erf(hv * (2.0 ** -0.5)))
        a_ref[pl.ds(start, PEER_NKEYS), :] = (gelu * g).astype(BF16)
        return 0

    lax.fori_loop(0, rows_per_tile, body, 0)
    o_ref[...] += _dot(vt_ref[...], a_ref[...])


def _peer_experts(xt, u, vt, w1, w2, th):
    d, n = xt.shape
    ne = u.shape[0]
    t = min(512, n)
    rpt = 4
    et = rpt * PEER_NKEYS
    return pl.pallas_call(
        functools.partial(_peer_expert_kernel, rows_per_tile=rpt),
        grid=(n // t, ne // et),
        in_specs=[
            pl.BlockSpec((d, t), lambda i, e: (0, i)),
            pl.BlockSpec((et, d), lambda i, e: (e, 0)),
            pl.BlockSpec((d, et), lambda i, e: (0, e)),
            pl.BlockSpec((PEER_HEADS, PEER_NKEYS, t), lambda i, e: (0, 0, i)),
            pl.BlockSpec((PEER_HEADS, PEER_NKEYS, t), lambda i, e: (0, 0, i)),
            pl.BlockSpec((PEER_HEADS, t), lambda i, e: (0, i)),
        ],
        out_specs=pl.BlockSpec((d, t), lambda i, e: (0, i)),
        out_shape=jax.ShapeDtypeStruct((d, n), F32),
        scratch_shapes=[pltpu.VMEM((et, t), F32), pltpu.VMEM((et, t), BF16)],
        compiler_params=_cparams("parallel", "arbitrary"),
        name="peer_experts",
    )(xt, u, vt, w1, w2, th)


def _residual_t_kernel(x_ref, yt_ref, o_ref):
    o_ref[...] = x_ref[...] + yt_ref[...].T


def _residual_t(x2, yt):
    n, d = x2.shape
    tm = 512
    return pl.pallas_call(
        _residual_t_kernel,
        grid=(n // tm,),
        in_specs=[pl.BlockSpec((tm, d), lambda i: (i, 0)), pl.BlockSpec((d, tm), lambda i: (0, i))],
        out_specs=pl.BlockSpec((tm, d), lambda i: (i, 0)),
        out_shape=jax.ShapeDtypeStruct((n, d), F32),
        compiler_params=_cparams("parallel"),
        name="residual_t",
    )(x2, yt)


def _final_norm_kernel(x_ref, g_ref, o_ref):
    x = x_ref[...]
    y = x * lax.rsqrt(jnp.mean(x * x, axis=-1, keepdims=True) + NORM_EPS)
    o_ref[...] = y * g_ref[...]


def _final_norm(x2, gain):
    n, d = x2.shape
    tm = 512
    return pl.pallas_call(
        _final_norm_kernel,
        grid=(n // tm,),
        in_specs=[pl.BlockSpec((tm, d), lambda i: (i, 0)), pl.BlockSpec((1, d), lambda i: (0, 0))],
        out_specs=pl.BlockSpec((tm, d), lambda i: (i, 0)),
        out_shape=jax.ShapeDtypeStruct((n, d), F32),
        compiler_params=_cparams("parallel"),
        name="final_norm",
    )(x2, gain)


def _attention_block(x2, batch, seq, rel_bias, attn_norm, w_in, diff_lambda, diff_subln, cmp_pos, cmp_w,
                     w_out, lam_init):
    d = x2.shape[1]
    n_main = _N_PROJ_HEADS * HEAD_DIM
    w_main = w_in[:, :n_main].astype(BF16)
    w_gate = jnp.pad(w_in[:, n_main:], ((0, 0), (0, LANES - (w_in.shape[1] - n_main)))).astype(BF16)
    proj, gates = _inproj(x2, attn_norm.reshape(1, d), w_main, w_gate, batch, seq)

    bias_moba = rel_bias[:, :N_MOBA]
    bias_diff = rel_bias[:, N_MOBA:N_MOBA + N_DIFF]
    bias_nsa = rel_bias[:, N_MOBA + N_DIFF:]

    blk = MOBA_BLOCK
    o_moba = _moba(proj, _bias_tiles(bias_moba, blk, blk, (0, blk, 2 * blk)))
    o_diff = _diff(proj, _bias_tiles(bias_diff, blk, blk, (0, blk, 2 * blk)), diff_lambda,
                   diff_subln.reshape(1, HEAD_DIM), lam_init)

    nrow = seq // CMP_STRIDE
    wid = CMP_STRIDE * HEAD_DIM
    xkv = proj[:, _NKC:_NKC + 2 * NSA_G].reshape(batch, 2 * NSA_G, nrow, wid)
    w_cmp = cmp_w.reshape(2, 2, wid, HEAD_DIM).astype(BF16)
    pos = jnp.pad(cmp_pos.reshape(2, 2, wid), ((0, 0), (0, 6), (0, 0))).astype(BF16)
    kvc = _nsa_compress(xkv, w_cmp, pos)

    n_sel = seq // SEL_BLOCK
    cmp_start = jnp.arange(nrow) * CMP_STRIDE
    sel_start = jnp.arange(LANES) * SEL_BLOCK
    overlap = ((cmp_start[:, None] <= (sel_start + SEL_BLOCK - 1)[None, :])
               & ((cmp_start + CMP_LEN - 1)[:, None] >= sel_start[None, :])
               & (jnp.arange(nrow) < (seq - CMP_LEN) // CMP_STRIDE + 1)[:, None]
               & (jnp.arange(LANES) < n_sel)[None, :]).astype(BF16)
    o_cmp, selneg = _nsa_cmp(proj, kvc, overlap)

    tq, tk = 128, 256
    expand = (jnp.arange(LANES)[:, None] == (jnp.arange(seq) // SEL_BLOCK)[None, :]).astype(BF16)
    expand = expand.reshape(LANES, seq // tk, tk).transpose(1, 0, 2)
    bt_sel = _bias_tiles(bias_nsa, tq, tk, (0, tq, 2 * tq, 3 * tq))
    bt_sel = bt_sel.reshape(NSA_G, NSA_R, 4, tq, tk).transpose(0, 2, 1, 3, 4).reshape(NSA_G, 4, NSA_R * tq, tk)
    o_sel = _nsa_sel(proj, selneg, expand, bt_sel)

    nband = NSA_WINDOW // tq + 1
    bt_win = _bias_tiles(bias_nsa, tq, tq, tuple(tq * m for m in range(nband)), lo=0, hi=NSA_WINDOW)
    bt_win = bt_win.reshape(NSA_G, NSA_R, nband, tq, tq).transpose(0, 2, 1, 3, 4).reshape(
        NSA_G, nband, NSA_R * tq, tq)
    n_gate = N_NSA * 3
    gates_g = gates[:, :n_gate].reshape(batch, seq, NSA_G, n_gate // NSA_G).transpose(0, 2, 1, 3)
    gates_g = jnp.pad(gates_g, ((0, 0), (0, 0), (0, 0), (0, LANES - n_gate // NSA_G)))
    o_nsa = _nsa_win_combine(proj, bt_win, o_cmp, o_sel, gates_g)

    return _outproj(o_moba, o_diff, o_nsa, w_out.astype(BF16), x2)


def _peer_block(x2, ffn_norm, wq, subkeys, u_tab, v_tab):
    d = x2.shape[1]
    sk = subkeys.reshape(PEER_HEADS * 2, PEER_NKEYS, subkeys.shape[-1]).astype(BF16)
    sc, xt = _peer_query(x2, ffn_norm.reshape(1, d), wq.astype(BF16), sk)
    w1, w2, th = _peer_route(sc)
    yt = _peer_experts(xt, u_tab.astype(BF16), v_tab.astype(BF16).T, w1, w2, th)
    return _residual_t(x2, yt)


def kernel(x, rel_bias, attn_norm, w_in, diff_lambda, diff_subln, nsa_cmp_pos, nsa_cmp_w, w_out, ffn_norm,
           peer_wq, peer_subkeys, peer_u, peer_v, final_norm):
    batch, seq, d = x.shape
    x2 = x.reshape(batch * seq, d)
    for l in range(attn_norm.shape[0]):
        lam_init = 0.8 - 0.6 * math.exp(-0.3 * l)
        x2 = _attention_block(x2, batch, seq, rel_bias, attn_norm[l], w_in[l], diff_lambda[l], diff_subln[l],
                              nsa_cmp_pos[l], nsa_cmp_w[l], w_out[l], lam_init)
        x2 = _peer_block(x2, ffn_norm[l], peer_wq[l], peer_subkeys[l], peer_u[l], peer_v[l])
    return _final_norm(x2, final_norm.reshape(1, d)).reshape(batch, seq, d)
```

```python
import functools
import math

import jax
import jax.numpy as jnp
from jax import lax
from jax.experimental import pallas as pl
from jax.experimental.pallas import tpu as pltpu

F32 = jnp.float32
BF16 = jnp.bfloat16
NEG = -1e30

HEAD_DIM = 128
LANES = 128
VMEM_LIMIT = 56 * 1024 * 1024

N_MOBA = 4
N_DIFF = 4
N_NSA = 8
NSA_G = 2
NSA_R = 4
MOBA_BLOCK = 256
MOBA_TOPK = 3
DIFF_EPS = 1e-5
NORM_EPS = 1e-6
CMP_LEN = 32
CMP_STRIDE = 16
SEL_BLOCK = 64
SEL_TOPK = 16
NSA_WINDOW = 512
NSA_FORCE = 1e6
FLASH_TK = 1024
NSA_TQ = 128
NSA_SEL_TK = 512
REL_BUCKETS = 32
REL_MAX_EXACT = 16
REL_MAX_DIST = 128
PEER_HEADS = 8
PEER_NKEYS = 128
PEER_TOPK = 16

_MQ, _MK, _MV = 0, 4, 8
_DQ, _DK, _DV = 12, 16, 20
_NQ = 24
_NKC, _NVC, _NKS, _NVS, _NKW, _NVW = 32, 34, 36, 38, 40, 42
_N_PROJ_HEADS = 44

_NT = (((1,), (1,)), ((), ()))


def _cparams(*sem):
    return pltpu.CompilerParams(dimension_semantics=sem, vmem_limit_bytes=VMEM_LIMIT)


def _dot(a, b):
    return jnp.dot(a, b, preferred_element_type=F32)


def _dot_nt(a, b):
    return lax.dot_general(a, b, _NT, preferred_element_type=F32)


def _t5_bucket(dist):
    n = jnp.maximum(dist, 0)
    nf = jnp.maximum(n, REL_MAX_EXACT).astype(F32)
    large = REL_MAX_EXACT + (jnp.log(nf / REL_MAX_EXACT) / math.log(REL_MAX_DIST / REL_MAX_EXACT)
                             * (REL_BUCKETS - REL_MAX_EXACT)).astype(jnp.int32)
    large = jnp.minimum(large, REL_BUCKETS - 1)
    return jnp.where(n < REL_MAX_EXACT, n, large)


def _bias_tiles(tab, tq, tk, offsets, lo=0, hi=None):
    t = jnp.arange(tq)[:, None]
    c = jnp.arange(tk)[None, :]
    rel = tab.astype(F32) - tab[REL_BUCKETS - 1:].astype(F32)
    tiles = []
    for off in offsets:
        dist = off + t - c
        onehot = (_t5_bucket(dist)[..., None] == jnp.arange(REL_BUCKETS)).astype(F32)
        b = jnp.einsum('tcb,bh->htc', onehot, rel, precision=lax.Precision.HIGHEST)
        ok = dist >= lo
        if hi is not None:
            ok = ok & (dist < hi)
        tiles.append(jnp.where(ok[None], b, NEG))
    return jnp.stack(tiles, axis=1).astype(F32)


def _inproj_kernel(x_ref, g_ref, w_ref, wg_ref, o_ref, go_ref, xn_ref, *, heads_per_tile):
    @pl.when(pl.program_id(1) == 0)
    def _():
        x = x_ref[...]
        y = x * lax.rsqrt(jnp.mean(x * x, axis=-1, keepdims=True) + NORM_EPS)
        xn_ref[...] = (y * g_ref[...]).astype(BF16)
        go_ref[...] = _dot(xn_ref[...], wg_ref[...])

    acc = _dot(xn_ref[...], w_ref[...])
    for hh in range(heads_per_tile):
        o_ref[0, hh] = acc[:, hh * HEAD_DIM:(hh + 1) * HEAD_DIM].astype(BF16)


def _inproj(x2, gain, w, wg, batch, seq):
    n, d = x2.shape
    e = w.shape[1]
    tm, tn = 512, 512
    hpt = tn // HEAD_DIM
    spt = seq // tm
    return pl.pallas_call(
        functools.partial(_inproj_kernel, heads_per_tile=hpt),
        grid=(n // tm, e // tn),
        in_specs=[
            pl.BlockSpec((tm, d), lambda i, j: (i, 0)),
            pl.BlockSpec((1, d), lambda i, j: (0, 0)),
            pl.BlockSpec((d, tn), lambda i, j: (0, j)),
            pl.BlockSpec((d, LANES), lambda i, j: (0, 0)),
        ],
        out_specs=[
            pl.BlockSpec((1, hpt, tm, HEAD_DIM), lambda i, j: (i // spt, j, i % spt, 0)),
            pl.BlockSpec((tm, LANES), lambda i, j: (i, 0)),
        ],
        out_shape=[
            jax.ShapeDtypeStruct((batch, e // HEAD_DIM, seq, HEAD_DIM), BF16),
            jax.ShapeDtypeStruct((n, LANES), F32),
        ],
        scratch_shapes=[pltpu.VMEM((tm, d), BF16)],
        compiler_params=_cparams("parallel", "arbitrary"),
        name="inproj",
    )(x2, gain, w, wg)


def _lane_fold(x, op):
    r = x[:, :LANES]
    for c in range(1, x.shape[1] // LANES):
        r = op(r, x[:, c * LANES:(c + 1) * LANES])
    return r


def _softmax_attend(q, n_tiles, k_tile, v_tile, extra, s_ref, acc_ref, scale):
    rows = q.shape[0]

    def scores(kt, mrun):
        s = _dot_nt(q, k_tile(kt)) * scale + extra(kt)
        s_ref[kt] = s
        return jnp.maximum(mrun, _lane_fold(s, jnp.maximum))

    mrun = lax.fori_loop(0, n_tiles, scores, jnp.full((rows, LANES), NEG, F32))
    m = jnp.broadcast_to(jnp.max(mrun, axis=1, keepdims=True), (rows, LANES))
    acc_ref[...] = jnp.zeros_like(acc_ref)

    def attend(kt, lrun):
        s = s_ref[kt]
        p = jnp.concatenate([jnp.exp(s[:, c * LANES:(c + 1) * LANES] - m) for c in range(s.shape[1] // LANES)],
                            axis=1)
        acc_ref[...] += _dot(p.astype(BF16), v_tile(kt))
        return lrun + _lane_fold(p, jnp.add)

    lrun = lax.fori_loop(0, n_tiles, attend, jnp.zeros((rows, LANES), F32))
    return acc_ref[...] / jnp.sum(lrun, axis=1, keepdims=True)


def _rank_before(vals, ncols):
    col = lax.broadcasted_iota(jnp.int32, vals.shape, 1)
    rank = jnp.zeros(vals.shape, F32)
    for c in range(ncols):
        vc = vals[:, c:c + 1]
        tie = jnp.where(col > c, 1.0, 0.0)
        rank = rank + jnp.where(vc > vals, 1.0, jnp.where(vc == vals, tie, 0.0))
    return rank


def _moba_kernel(q_ref, k_ref, v_ref, bt_ref, o_ref, kmean_ref, s_ref, acc_ref, *, scale, nblk):
    i = pl.program_id(2)
    tq = q_ref.shape[2]
    tk = s_ref.shape[2]
    per = tk // tq

    @pl.when(i == 0)
    def _():
        kmean_ref[...] = jnp.zeros_like(kmean_ref)
        kf = k_ref[0, 0].astype(F32).reshape(nblk, tq, HEAD_DIM)
        kmean_ref[0:nblk, :] = jnp.mean(kf, axis=1)

    q = q_ref[0, 0]
    col = lax.broadcasted_iota(jnp.int32, (tq, LANES), 1)
    gate = _dot_nt(q, kmean_ref[...].astype(BF16))
    gate = jnp.where(col < i, gate, NEG)
    rank = _rank_before(gate, nblk)
    selneg = jnp.where(((rank < MOBA_TOPK) & (col < i)) | (col == i), 0.0, NEG)

    def extra(kt):
        bias = bt_ref[0, jnp.minimum(i - per * kt, bt_ref.shape[1] - 1)]
        parts = []
        for hb in range(per):
            selcol = jnp.sum(jnp.where(col == per * kt + hb, selneg, 0.0), axis=1, keepdims=True)
            parts.append(bias[:, hb * tq:(hb + 1) * tq] + selcol)
        return jnp.concatenate(parts, axis=1)

    out = _softmax_attend(
        q, i // per + 1,
        lambda kt: k_ref[0, 0, pl.ds(pl.multiple_of(kt * tk, tk), tk), :],
        lambda kt: v_ref[0, 0, pl.ds(pl.multiple_of(kt * tk, tk), tk), :],
        extra, s_ref, acc_ref, scale)
    o_ref[0, 0] = out.astype(o_ref.dtype)


def _moba(proj, bias_tiles):
    b, _, s, _ = proj.shape
    tq = MOBA_BLOCK
    tk = bias_tiles.shape[3]
    nblk = s // tq
    return pl.pallas_call(
        functools.partial(_moba_kernel, scale=HEAD_DIM ** -0.5, nblk=nblk),
        grid=(b, N_MOBA, nblk),
        in_specs=[
            pl.BlockSpec((1, 1, tq, HEAD_DIM), lambda bb, h, i: (bb, _MQ + h, i, 0)),
            pl.BlockSpec((1, 1, s, HEAD_DIM), lambda bb, h, i: (bb, _MK + h, 0, 0)),
            pl.BlockSpec((1, 1, s, HEAD_DIM), lambda bb, h, i: (bb, _MV + h, 0, 0)),
            pl.BlockSpec((1,) + bias_tiles.shape[1:], lambda bb, h, i: (h, 0, 0, 0)),
        ],
        out_specs=pl.BlockSpec((1, 1, tq, HEAD_DIM), lambda bb, h, i: (bb, h, i, 0)),
        out_shape=jax.ShapeDtypeStruct((b, N_MOBA, s, HEAD_DIM), BF16),
        scratch_shapes=[pltpu.VMEM((LANES, HEAD_DIM), F32), pltpu.VMEM((s // tk, tq, tk), F32),
                        pltpu.VMEM((tq, HEAD_DIM), F32)],
        compiler_params=_cparams("parallel", "parallel", "arbitrary"),
        name="moba",
    )(proj, proj, proj, bias_tiles)


def _diff_kernel(q_ref, k_ref, v_ref, bt_ref, lam_ref, gain_ref, o_ref, s_ref, acc_ref, *, scale, lam_init):
    i = pl.program_id(2)
    tq = q_ref.shape[2]
    tk = s_ref.shape[2]
    per = tk // tq
    q = q_ref[0, 0]
    lane = lax.broadcasted_iota(jnp.int32, q.shape, 1)
    half = HEAD_DIM // 2

    def attend(qh):
        return _softmax_attend(
            qh, i // per + 1,
            lambda kt: k_ref[0, 0, pl.ds(pl.multiple_of(kt * tk, tk), tk), :],
            lambda kt: v_ref[0, 0, pl.ds(pl.multiple_of(kt * tk, tk), tk), :],
            lambda kt: bt_ref[0, jnp.minimum(i - per * kt, bt_ref.shape[1] - 1)],
            s_ref, acc_ref, scale)

    o1 = attend(jnp.where(lane < half, q, jnp.zeros_like(q)))
    o2 = attend(jnp.where(lane >= half, q, jnp.zeros_like(q)))

    lp = lam_ref[...]
    lam = (jnp.exp(jnp.sum(lp[0:1] * lp[1:2], axis=1, keepdims=True))
           - jnp.exp(jnp.sum(lp[2:3] * lp[3:4], axis=1, keepdims=True)) + lam_init)
    o = o1 - lam * o2
    y = o * lax.rsqrt(jnp.mean(o * o, axis=-1, keepdims=True) + DIFF_EPS)
    o_ref[0, 0] = ((y * gain_ref[...]) * (1.0 - lam_init)).astype(o_ref.dtype)


def _diff(proj, bias_tiles, lam_params, sub_gain, lam_init):
    b, _, s, _ = proj.shape
    tq = bias_tiles.shape[2]
    tk = bias_tiles.shape[3]
    return pl.pallas_call(
        functools.partial(_diff_kernel, scale=(HEAD_DIM // 2) ** -0.5, lam_init=lam_init),
        grid=(b, N_DIFF, s // tq),
        in_specs=[
            pl.BlockSpec((1, 1, tq, HEAD_DIM), lambda bb, h, i: (bb, _DQ + h, i, 0)),
            pl.BlockSpec((1, 1, s, HEAD_DIM), lambda bb, h, i: (bb, _DK + h, 0, 0)),
            pl.BlockSpec((1, 1, s, HEAD_DIM), lambda bb, h, i: (bb, _DV + h, 0, 0)),
            pl.BlockSpec((1,) + bias_tiles.shape[1:], lambda bb, h, i: (h, 0, 0, 0)),
            pl.BlockSpec(lam_params.shape, lambda bb, h, i: (0, 0)),
            pl.BlockSpec((1, HEAD_DIM), lambda bb, h, i: (0, 0)),
        ],
        out_specs=pl.BlockSpec((1, 1, tq, HEAD_DIM), lambda bb, h, i: (bb, h, i, 0)),
        out_shape=jax.ShapeDtypeStruct((b, N_DIFF, s, HEAD_DIM), BF16),
        scratch_shapes=[pltpu.VMEM((s // tk, tq, tk), F32), pltpu.VMEM((tq, HEAD_DIM), F32)],
        compiler_params=_cparams("parallel", "parallel", "arbitrary"),
        name="diff",
    )(proj, proj, proj, bias_tiles, lam_params, sub_gain)


def _nsa_compress_kernel(x_ref, w_ref, pos_ref, o_ref):
    x = x_ref[0, 0]
    w0, w1 = w_ref[0, 0], w_ref[0, 1]
    y0 = _dot(x, w0)
    y1 = _dot(x, w1)
    pv = _dot(pos_ref[0], w0)[0:1] + _dot(pos_ref[0], w1)[1:2]
    n = y1.shape[0]
    y1_next = pltpu.roll(y1, n - 1, 0)
    o_ref[0, 0] = (y0 + y1_next + pv).astype(o_ref.dtype)


def _nsa_compress(xkv, w, pos):
    b, hh, nrow, wid = xkv.shape
    return pl.pallas_call(
        _nsa_compress_kernel,
        grid=(b, hh),
        in_specs=[
            pl.BlockSpec((1, 1, nrow, wid), lambda bb, h: (bb, h, 0, 0)),
            pl.BlockSpec((1, 2, wid, HEAD_DIM), lambda bb, h: (h // NSA_G, 0, 0, 0)),
            pl.BlockSpec((1, 8, wid), lambda bb, h: (h // NSA_G, 0, 0)),
        ],
        out_specs=pl.BlockSpec((1, 1, nrow, HEAD_DIM), lambda bb, h: (bb, h, 0, 0)),
        out_shape=jax.ShapeDtypeStruct((b, hh, nrow, HEAD_DIM), BF16),
        compiler_params=_cparams("parallel", "parallel"),
        name="nsa_compress",
    )(xkv, w, pos)


def _nsa_cmp_kernel(q_ref, kc_ref, vc_ref, ov_ref, o_ref, sel_ref, *, scale, nsel):
    i = pl.program_id(2)
    tq = q_ref.shape[2]
    ncmp = kc_ref.shape[2]
    kc = kc_ref[0, 0]
    vc = vc_ref[0, 0]
    ov = ov_ref[...]
    pos = i * tq + lax.broadcasted_iota(jnp.int32, (tq, ncmp), 0)
    ncol = lax.broadcasted_iota(jnp.int32, (tq, ncmp), 1)
    ok = (ncol * CMP_STRIDE + (CMP_LEN - 1)) <= pos
    imp = jnp.zeros((tq, LANES), F32)
    for r in range(NSA_R):
        s = jnp.where(ok, _dot_nt(q_ref[0, r], kc) * scale, NEG)
        m = jnp.max(s, axis=1, keepdims=True)
        p = jnp.where(ok, jnp.exp(s - m), 0.0)
        l = jnp.sum(p, axis=1, keepdims=True)
        pn = p / jnp.where(l > 0.0, l, 1.0)
        hi = pn.astype(BF16)
        lo = (pn - hi.astype(F32)).astype(BF16)
        o_ref[0, r] = _dot(hi, vc)
        imp = imp + _dot(hi, ov) + _dot(lo, ov)

    jj = lax.broadcasted_iota(jnp.int32, (tq, LANES), 1)
    own = (i * tq + lax.broadcasted_iota(jnp.int32, (tq, LANES), 0)) // SEL_BLOCK
    forced = (jj == 0) | (jj == own) | (jj == own - 1)
    visible = jj <= own
    imp = jnp.where(visible, jnp.where(forced, NSA_FORCE, imp), NEG)
    rank = _rank_before(imp, nsel)
    sel_ref[0, 0] = jnp.where((rank < SEL_TOPK) & visible, 0.0, NEG)


def _nsa_cmp(proj, kvc, overlap):
    b, _, s, _ = proj.shape
    tq = 128
    ncmp = kvc.shape[2]
    return pl.pallas_call(
        functools.partial(_nsa_cmp_kernel, scale=HEAD_DIM ** -0.5, nsel=s // SEL_BLOCK),
        grid=(b, NSA_G, s // tq),
        in_specs=[
            pl.BlockSpec((1, NSA_R, tq, HEAD_DIM), lambda bb, g, i: (bb, _NQ // NSA_R + g, i, 0)),
            pl.BlockSpec((1, 1, ncmp, HEAD_DIM), lambda bb, g, i: (bb, g, 0, 0)),
            pl.BlockSpec((1, 1, ncmp, HEAD_DIM), lambda bb, g, i: (bb, NSA_G + g, 0, 0)),
            pl.BlockSpec(overlap.shape, lambda bb, g, i: (0, 0)),
        ],
        out_specs=[
            pl.BlockSpec((1, NSA_R, tq, HEAD_DIM), lambda bb, g, i: (bb, g, i, 0)),
            pl.BlockSpec((1, 1, tq, LANES), lambda bb, g, i: (bb, g, i, 0)),
        ],
        out_shape=[
            jax.ShapeDtypeStruct((b, N_NSA, s, HEAD_DIM), F32),
            jax.ShapeDtypeStruct((b, NSA_G, s, LANES), F32),
        ],
        compiler_params=_cparams("parallel", "parallel", "parallel"),
        name="nsa_cmp",
    )(proj, kvc, kvc, overlap)


def _nsa_sel_kernel(q_ref, k_ref, v_ref, sel_ref, ex_ref, bt_ref, o_ref, s_ref, acc_ref, *, scale):
    i = pl.program_id(2)
    tq = q_ref.shape[2]
    tk = ex_ref.shape[2]
    rows = NSA_R * tq
    q = q_ref[0].reshape(rows, HEAD_DIM)
    selneg = sel_ref[0, 0].astype(BF16)
    ratio = tk // tq

    def extra(kt):
        mask = _dot(selneg, ex_ref[kt])
        return (bt_ref[0, jnp.minimum(i - ratio * kt, bt_ref.shape[1] - 1)]
                + jnp.concatenate([mask] * NSA_R, axis=0))

    out = _softmax_attend(
        q, i // ratio + 1,
        lambda kt: k_ref[0, 0, pl.ds(pl.multiple_of(kt * tk, tk), tk), :],
        lambda kt: v_ref[0, 0, pl.ds(pl.multiple_of(kt * tk, tk), tk), :],
        extra, s_ref, acc_ref, scale)
    o_ref[0] = out.reshape(NSA_R, tq, HEAD_DIM)


def _nsa_sel(proj, selneg, expand, bias_tiles):
    b, _, s, _ = proj.shape
    tq = 128
    tk = expand.shape[2]
    rows = NSA_R * tq
    return pl.pallas_call(
        functools.partial(_nsa_sel_kernel, scale=HEAD_DIM ** -0.5),
        grid=(b, NSA_G, s // tq),
        in_specs=[
            pl.BlockSpec((1, NSA_R, tq, HEAD_DIM), lambda bb, g, i: (bb, _NQ // NSA_R + g, i, 0)),
            pl.BlockSpec((1, 1, s, HEAD_DIM), lambda bb, g, i: (bb, _NKS + g, 0, 0)),
            pl.BlockSpec((1, 1, s, HEAD_DIM), lambda bb, g, i: (bb, _NVS + g, 0, 0)),
            pl.BlockSpec((1, 1, tq, LANES), lambda bb, g, i: (bb, g, i, 0)),
            pl.BlockSpec(expand.shape, lambda bb, g, i: (0, 0, 0)),
            pl.BlockSpec((1,) + bias_tiles.shape[1:], lambda bb, g, i: (g, 0, 0, 0)),
        ],
        out_specs=pl.BlockSpec((1, NSA_R, tq, HEAD_DIM), lambda bb, g, i: (bb, g, i, 0)),
        out_shape=jax.ShapeDtypeStruct((b, N_NSA, s, HEAD_DIM), F32),
        scratch_shapes=[pltpu.VMEM((s // tk, rows, tk), F32), pltpu.VMEM((rows, HEAD_DIM), F32)],
        compiler_params=_cparams("parallel", "parallel", "arbitrary"),
        name="nsa_sel",
    )(proj, proj, proj, selneg, expand, bias_tiles)


def _nsa_win_kernel(q_ref, k_ref, v_ref, bt_ref, oc_ref, os_ref, gate_ref, o_ref, *, scale, nband):
    i = pl.program_id(2)
    tq = q_ref.shape[2]
    rows = NSA_R * tq
    q = q_ref[0].reshape(rows, HEAD_DIM)
    scores, values = [], []
    for m in range(nband):
        kt = i - m
        start = pl.multiple_of(jnp.maximum(kt, 0) * tq, tq)
        s = _dot_nt(q, k_ref[0, 0, pl.ds(start, tq), :]) * scale + bt_ref[0, m]
        scores.append(jnp.where(kt >= 0, s, NEG))
        values.append(v_ref[0, 0, pl.ds(start, tq), :])
    mx = scores[0]
    for s in scores[1:]:
        mx = jnp.maximum(mx, s)
    mx = jnp.broadcast_to(jnp.max(mx, axis=1, keepdims=True), (rows, LANES))
    lsum = jnp.zeros((rows, LANES), F32)
    acc = jnp.zeros((rows, HEAD_DIM), F32)
    for s, v in zip(scores, values):
        p = jnp.exp(s - mx)
        lsum = lsum + p
        acc = acc + _dot(p.astype(BF16), v)
    ow = (acc / jnp.sum(lsum, axis=1, keepdims=True)).reshape(NSA_R, tq, HEAD_DIM)
    g = jax.nn.sigmoid(gate_ref[0, 0])
    for r in range(NSA_R):
        gc, gs, gw = (g[:, 3 * r + br:3 * r + br + 1] for br in range(3))
        o_ref[0, r] = (gc * oc_ref[0, r] + gs * os_ref[0, r] + gw * ow[r]).astype(o_ref.dtype)


def _nsa_win_combine(proj, bias_tiles, o_cmp, o_sel, gates3):
    b, _, s, _ = proj.shape
    tq = 128
    rows = NSA_R * tq
    nband = bias_tiles.shape[1]
    blk = pl.BlockSpec((1, NSA_R, tq, HEAD_DIM), lambda bb, g, i: (bb, g, i, 0))
    return pl.pallas_call(
        functools.partial(_nsa_win_kernel, scale=HEAD_DIM ** -0.5, nband=nband),
        grid=(b, NSA_G, s // tq),
        in_specs=[
            pl.BlockSpec((1, NSA_R, tq, HEAD_DIM), lambda bb, g, i: (bb, _NQ // NSA_R + g, i, 0)),
            pl.BlockSpec((1, 1, s, HEAD_DIM), lambda bb, g, i: (bb, _NKW + g, 0, 0)),
            pl.BlockSpec((1, 1, s, HEAD_DIM), lambda bb, g, i: (bb, _NVW + g, 0, 0)),
            pl.BlockSpec((1, nband, rows, tq), lambda bb, g, i: (g, 0, 0, 0)),
            blk, blk,
            pl.BlockSpec((1, 1, tq, LANES), lambda bb, g, i: (bb, g, i, 0)),
        ],
        out_specs=blk,
        out_shape=jax.ShapeDtypeStruct((b, N_NSA, s, HEAD_DIM), BF16),
        compiler_params=_cparams("parallel", "parallel", "parallel"),
        name="nsa_win",
    )(proj, proj, proj, bias_tiles, o_cmp, o_sel, gates3)


def _outproj_kernel(om_ref, od_ref, on_ref, w_ref, x_ref, o_ref, cat_ref):
    h = 0
    for ref in (om_ref, od_ref, on_ref):
        for hh in range(ref.shape[1]):
            cat_ref[:, h * HEAD_DIM:(h + 1) * HEAD_DIM] = ref[0, hh]
            h += 1
    o_ref[...] = x_ref[...] + _dot(cat_ref[...], w_ref[...])


def _outproj(o_moba, o_diff, o_nsa, w, x2):
    n, d = x2.shape
    b, _, s, _ = o_moba.shape
    tm = 512
    spt = s // tm

    def hspec(nh):
        return pl.BlockSpec((1, nh, tm, HEAD_DIM), lambda i: (i // spt, 0, i % spt, 0))

    return pl.pallas_call(
        _outproj_kernel,
        grid=(n // tm,),
        in_specs=[hspec(N_MOBA), hspec(N_DIFF), hspec(N_NSA),
                  pl.BlockSpec(w.shape, lambda i: (0, 0)),
                  pl.BlockSpec((tm, d), lambda i: (i, 0))],
        out_specs=pl.BlockSpec((tm, d), lambda i: (i, 0)),
        out_shape=jax.ShapeDtypeStruct((n, d), F32),
        scratch_shapes=[pltpu.VMEM((tm, w.shape[0]), BF16)],
        compiler_params=_cparams("parallel"),
        name="outproj",
    )(o_moba, o_diff, o_nsa, w, x2)


def _peer_query_kernel(x_ref, g_ref, wq_ref, sk_ref, sc_ref, xt_ref):
    x = x_ref[...]
    y = x * lax.rsqrt(jnp.mean(x * x, axis=-1, keepdims=True) + NORM_EPS)
    xn = y * g_ref[...]
    xt_ref[...] = xn.T.astype(BF16)
    qry = _dot(xn.astype(BF16), wq_ref[...]).astype(BF16)
    for hp in range(sk_ref.shape[0]):
        sc_ref[hp] = _dot_nt(sk_ref[hp], qry[:, hp * PEER_NKEYS:(hp + 1) * PEER_NKEYS])


def _peer_query(x2, gain, wq, subkeys):
    n, d = x2.shape
    tm = 512
    nhp = subkeys.shape[0]
    return pl.pallas_call(
        _peer_query_kernel,
        grid=(n // tm,),
        in_specs=[
            pl.BlockSpec((tm, d), lambda i: (i, 0)),
            pl.BlockSpec((1, d), lambda i: (0, 0)),
            pl.BlockSpec(wq.shape, lambda i: (0, 0)),
            pl.BlockSpec(subkeys.shape, lambda i: (0, 0, 0)),
        ],
        out_specs=[
            pl.BlockSpec((nhp, PEER_NKEYS, tm), lambda i: (0, 0, i)),
            pl.BlockSpec((d, tm), lambda i: (0, i)),
        ],
        out_shape=[
            jax.ShapeDtypeStruct((nhp, PEER_NKEYS, n), F32),
            jax.ShapeDtypeStruct((d, n), BF16),
        ],
        compiler_params=_cparams("parallel"),
        name="peer_query",
    )(x2, gain, wq, subkeys)


def _top_sorted(a, k):
    rowid = lax.broadcasted_iota(jnp.int32, a.shape, 0)
    nrow = a.shape[0]
    member = jnp.zeros(a.shape, F32)
    vals = []
    for _ in range(k):
        m = jnp.max(a, axis=0, keepdims=True)
        first = jnp.min(jnp.where(a == m, rowid, nrow), axis=0, keepdims=True)
        hit = rowid == first
        member = jnp.where(hit, 1.0, member)
        a = jnp.where(hit, NEG, a)
        vals.append(m)
    return vals, member


def _peer_route_kernel(sc_ref, w1_ref, w2_ref, th_ref):
    k = PEER_TOPK
    for h in range(PEER_HEADS):
        a = sc_ref[2 * h]
        b = sc_ref[2 * h + 1]
        av, amem = _top_sorted(a, k)
        bv, bmem = _top_sorted(b, k)
        pairs = [(r, c) for r in range(k) for c in range(k) if (r + 1) * (c + 1) <= k + 1]
        nrow = -(-len(pairs) // 8) * 8
        crow = lax.broadcasted_iota(jnp.int32, (nrow, a.shape[1]), 0)
        cmat = jnp.full((nrow, a.shape[1]), NEG, F32)
        for idx, (r, c) in enumerate(pairs):
            cmat = jnp.where(crow == idx, av[r] + bv[c], cmat)
        top, _ = _top_sorted(cmat, k + 1)
        peak = top[0]
        z = jnp.zeros_like(peak)
        for t in top[:k]:
            z = z + jnp.exp(t - peak)
        mid = 0.5 * (top[k - 1] + top[k])
        th_ref[h:h + 1, :] = jnp.exp(mid - peak) / z
        w1_ref[h] = jnp.where(amem > 0.0, jnp.exp(a - av[0]), 0.0)
        w2_ref[h] = jnp.where(bmem > 0.0, jnp.exp(b - bv[0]), 0.0) / z


def _peer_route(sc):
    nhp, nk, n = sc.shape
    t = 256
    return pl.pallas_call(
        _peer_route_kernel,
        grid=(n // t,),
        in_specs=[pl.BlockSpec((nhp, nk, t), lambda i: (0, 0, i))],
        out_specs=[
            pl.BlockSpec((PEER_HEADS, nk, t), lambda i: (0, 0, i)),
            pl.BlockSpec((PEER_HEADS, nk, t), lambda i: (0, 0, i)),
            pl.BlockSpec((PEER_HEADS, t), lambda i: (0, i)),
        ],
        out_shape=[
            jax.ShapeDtypeStruct((PEER_HEADS, nk, n), F32),
            jax.ShapeDtypeStruct((PEER_HEADS, nk, n), F32),
            jax.ShapeDtypeStruct((PEER_HEADS, n), F32),
        ],
        compiler_params=_cparams("parallel"),
        name="peer_route",
    )(sc)


def _peer_expert_kernel(xt_ref, u_ref, vt_ref, w1_ref, w2_ref, th_ref, o_ref, a_ref, *, rows_per_tile, n_tiles):
    e = pl.program_id(1)

    @pl.when(e == 0)
    def _():
        o_ref[...] = jnp.zeros_like(o_ref)
        a_ref[...] = jnp.zeros_like(a_ref)

    slot = e % 2
    row0 = jnp.minimum(e, n_tiles - 1) * rows_per_tile
    a_prev = a_ref[1 - slot]
    orows = o_ref.shape[0] // rows_per_tile
    for ii in range(rows_per_tile):
        hv = _dot(u_ref[ii * PEER_NKEYS:(ii + 1) * PEER_NKEYS, :], xt_ref[...])
        o_ref[ii * orows:(ii + 1) * orows, :] += _dot(vt_ref[ii * orows:(ii + 1) * orows, :], a_prev)
        wrows = [w1_ref[hd, pl.ds(row0 + ii, 1), :] for hd in range(PEER_HEADS)]
        for lt in range(xt_ref.shape[1] // LANES):
            ls = slice(lt * LANES, (lt + 1) * LANES)
            g = None
            for hd in range(PEER_HEADS):
                p = wrows[hd][:, ls] * w2_ref[hd, :, ls]
                term = jnp.where(p >= th_ref[hd:hd + 1, ls], p, 0.0)
                g = term if g is None else g + term
            hl = hv[:, ls]
            gelu = 0.5 * hl * (1.0 + lax.erf(hl * (2.0 ** -0.5)))
            a_ref[slot, ii * PEER_NKEYS:(ii + 1) * PEER_NKEYS, ls] = (gelu * g).astype(BF16)


def _peer_experts(xt, u, vt, w1, w2, th):
    d, n = xt.shape
    ne = u.shape[0]
    t = min(512, n)
    rpt = 4
    et = rpt * PEER_NKEYS
    n_tiles = ne // et
    return pl.pallas_call(
        functools.partial(_peer_expert_kernel, rows_per_tile=rpt, n_tiles=n_tiles),
        grid=(n // t, n_tiles + 1),
        in_specs=[
            pl.BlockSpec((d, t), lambda i, e: (0, i)),
            pl.BlockSpec((et, d), lambda i, e: (jnp.minimum(e, n_tiles - 1), 0)),
            pl.BlockSpec((d, et), lambda i, e: (0, jnp.maximum(e - 1, 0))),
            pl.BlockSpec((PEER_HEADS, PEER_NKEYS, t), lambda i, e: (0, 0, i)),
            pl.BlockSpec((PEER_HEADS, PEER_NKEYS, t), lambda i, e: (0, 0, i)),
            pl.BlockSpec((PEER_HEADS, t), lambda i, e: (0, i)),
        ],
        out_specs=pl.BlockSpec((d, t), lambda i, e: (0, i)),
        out_shape=jax.ShapeDtypeStruct((d, n), F32),
        scratch_shapes=[pltpu.VMEM((2, et, t), BF16)],
        compiler_params=_cparams("parallel", "arbitrary"),
        name="peer_experts",
    )(xt, u, vt, w1, w2, th)


def _residual_t_kernel(x_ref, yt_ref, o_ref):
    o_ref[...] = x_ref[...] + yt_ref[...].T


def _residual_t(x2, yt):
    n, d = x2.shape
    tm = 512
    return pl.pallas_call(
        _residual_t_kernel,
        grid=(n // tm,),
        in_specs=[pl.BlockSpec((tm, d), lambda i: (i, 0)), pl.BlockSpec((d, tm), lambda i: (0, i))],
        out_specs=pl.BlockSpec((tm, d), lambda i: (i, 0)),
        out_shape=jax.ShapeDtypeStruct((n, d), F32),
        compiler_params=_cparams("parallel"),
        name="residual_t",
    )(x2, yt)


def _final_norm_kernel(x_ref, g_ref, o_ref):
    x = x_ref[...]
    y = x * lax.rsqrt(jnp.mean(x * x, axis=-1, keepdims=True) + NORM_EPS)
    o_ref[...] = y * g_ref[...]


def _final_norm(x2, gain):
    n, d = x2.shape
    tm = 512
    return pl.pallas_call(
        _final_norm_kernel,
        grid=(n // tm,),
        in_specs=[pl.BlockSpec((tm, d), lambda i: (i, 0)), pl.BlockSpec((1, d), lambda i: (0, 0))],
        out_specs=pl.BlockSpec((tm, d), lambda i: (i, 0)),
        out_shape=jax.ShapeDtypeStruct((n, d), F32),
        compiler_params=_cparams("parallel"),
        name="final_norm",
    )(x2, gain)


def _position_tables(rel_bias, seq):
    bias_moba = rel_bias[:, :N_MOBA]
    bias_diff = rel_bias[:, N_MOBA:N_MOBA + N_DIFF]
    bias_nsa = rel_bias[:, N_MOBA + N_DIFF:]
    blk = MOBA_BLOCK
    offs = tuple(blk * m for m in range(FLASH_TK // blk + 2))
    tabs = {
        "moba": _bias_tiles(bias_moba, blk, FLASH_TK, offs),
        "diff": _bias_tiles(bias_diff, blk, FLASH_TK, offs),
    }
    tq, tk = NSA_TQ, NSA_SEL_TK
    n_off = tk // tq + 2
    bt_sel = _bias_tiles(bias_nsa, tq, tk, tuple(tq * m for m in range(n_off)))
    tabs["sel"] = bt_sel.reshape(NSA_G, NSA_R, n_off, tq, tk).transpose(0, 2, 1, 3, 4).reshape(
        NSA_G, n_off, NSA_R * tq, tk)
    nband = NSA_WINDOW // tq + 1
    bt_win = _bias_tiles(bias_nsa, tq, tq, tuple(tq * m for m in range(nband)), lo=0, hi=NSA_WINDOW)
    tabs["win"] = bt_win.reshape(NSA_G, NSA_R, nband, tq, tq).transpose(0, 2, 1, 3, 4).reshape(
        NSA_G, nband, NSA_R * tq, tq)

    nrow = seq // CMP_STRIDE
    n_sel = seq // SEL_BLOCK
    cmp_start = jnp.arange(nrow) * CMP_STRIDE
    sel_start = jnp.arange(LANES) * SEL_BLOCK
    tabs["overlap"] = ((cmp_start[:, None] <= (sel_start + SEL_BLOCK - 1)[None, :])
                       & ((cmp_start + CMP_LEN - 1)[:, None] >= sel_start[None, :])
                       & (jnp.arange(nrow) < (seq - CMP_LEN) // CMP_STRIDE + 1)[:, None]
                       & (jnp.arange(LANES) < n_sel)[None, :]).astype(BF16)
    expand = (jnp.arange(LANES)[:, None] == (jnp.arange(seq) // SEL_BLOCK)[None, :]).astype(BF16)
    tabs["expand"] = expand.reshape(LANES, seq // tk, tk).transpose(1, 0, 2)
    return tabs


def _attention_block(x2, batch, seq, tabs, attn_norm, w_in, diff_lambda, diff_subln, cmp_pos, cmp_w, w_out,
                     lam_init):
    d = x2.shape[1]
    n_main = _N_PROJ_HEADS * HEAD_DIM
    w_main = w_in[:, :n_main].astype(BF16)
    w_gate = jnp.pad(w_in[:, n_main:], ((0, 0), (0, LANES - (w_in.shape[1] - n_main)))).astype(BF16)
    proj, gates = _inproj(x2, attn_norm.reshape(1, d), w_main, w_gate, batch, seq)

    o_moba = _moba(proj, tabs["moba"])
    o_diff = _diff(proj, tabs["diff"], diff_lambda, diff_subln.reshape(1, HEAD_DIM), lam_init)

    nrow = seq // CMP_STRIDE
    wid = CMP_STRIDE * HEAD_DIM
    xkv = proj[:, _NKC:_NKC + 2 * NSA_G].reshape(batch, 2 * NSA_G, nrow, wid)
    w_cmp = cmp_w.reshape(2, 2, wid, HEAD_DIM).astype(BF16)
    pos = jnp.pad(cmp_pos.reshape(2, 2, wid), ((0, 0), (0, 6), (0, 0))).astype(BF16)
    kvc = _nsa_compress(xkv, w_cmp, pos)

    o_cmp, selneg = _nsa_cmp(proj, kvc, tabs["overlap"])
    o_sel = _nsa_sel(proj, selneg, tabs["expand"], tabs["sel"])

    n_gate = N_NSA * 3
    gates_g = gates[:, :n_gate].reshape(batch, seq, NSA_G, n_gate // NSA_G).transpose(0, 2, 1, 3)
    gates_g = jnp.pad(gates_g, ((0, 0), (0, 0), (0, 0), (0, LANES - n_gate // NSA_G)))
    o_nsa = _nsa_win_combine(proj, tabs["win"], o_cmp, o_sel, gates_g)

    return _outproj(o_moba, o_diff, o_nsa, w_out.astype(BF16), x2)


def _peer_block(x2, ffn_norm, wq, subkeys, u_tab, v_tab):
    d = x2.shape[1]
    sk = subkeys.reshape(PEER_HEADS * 2, PEER_NKEYS, subkeys.shape[-1]).astype(BF16)
    sc, xt = _peer_query(x2, ffn_norm.reshape(1, d), wq.astype(BF16), sk)
    w1, w2, th = _peer_route(sc)
    yt = _peer_experts(xt, u_tab.astype(BF16), v_tab.astype(BF16).T, w1, w2, th)
    return _residual_t(x2, yt)


def kernel(x, rel_bias, attn_norm, w_in, diff_lambda, diff_subln, nsa_cmp_pos, nsa_cmp_w, w_out, ffn_norm,
           peer_wq, peer_subkeys, peer_u, peer_v, final_norm):
    batch, seq, d = x.shape
    x2 = x.reshape(batch * seq, d)
    tabs = _position_tables(rel_bias, seq)
    for l in range(attn_norm.shape[0]):
        lam_init = 0.8 - 0.6 * math.exp(-0.3 * l)
        x2 = _attention_block(x2, batch, seq, tabs, attn_norm[l], w_in[l], diff_lambda[l], diff_subln[l],
                              nsa_cmp_pos[l], nsa_cmp_w[l], w_out[l], lam_init)
        x2 = _peer_block(x2, ffn_norm[l], peer_wq[l], peer_subkeys[l], peer_u[l], peer_v[l])
    return _final_norm(x2, final_norm.reshape(1, d)).reshape(batch, seq, d)
```

```python
import functools
import math

import jax
import jax.numpy as jnp
from jax import lax
from jax.experimental import pallas as pl
from jax.experimental.pallas import tpu as pltpu

F32 = jnp.float32
BF16 = jnp.bfloat16
NEG = -1e30

HEAD_DIM = 128
LANES = 128
VMEM_LIMIT = 56 * 1024 * 1024

N_MOBA = 4
N_DIFF = 4
N_NSA = 8
NSA_G = 2
NSA_R = 4
MOBA_BLOCK = 256
MOBA_TOPK = 3
DIFF_EPS = 1e-5
NORM_EPS = 1e-6
CMP_LEN = 32
CMP_STRIDE = 16
SEL_BLOCK = 64
SEL_TOPK = 16
NSA_WINDOW = 512
NSA_FORCE = 1e6
FLASH_TK = 1024
NSA_TQ = 128
NSA_SEL_TK = 512
REL_BUCKETS = 32
REL_MAX_EXACT = 16
REL_MAX_DIST = 128
PEER_HEADS = 8
PEER_NKEYS = 128
PEER_TOPK = 16

_MQ, _MK, _MV = 0, 4, 8
_DQ, _DK, _DV = 12, 16, 20
_NQ = 24
_NKC, _NVC, _NKS, _NVS, _NKW, _NVW = 32, 34, 36, 38, 40, 42
_N_PROJ_HEADS = 44

_NT = (((1,), (1,)), ((), ()))


def _cparams(*sem):
    return pltpu.CompilerParams(dimension_semantics=sem, vmem_limit_bytes=VMEM_LIMIT)


def _dot(a, b):
    return jnp.dot(a, b, preferred_element_type=F32)


def _dot_nt(a, b):
    return lax.dot_general(a, b, _NT, preferred_element_type=F32)


def _t5_bucket(dist):
    n = jnp.maximum(dist, 0)
    nf = jnp.maximum(n, REL_MAX_EXACT).astype(F32)
    large = REL_MAX_EXACT + (jnp.log(nf / REL_MAX_EXACT) / math.log(REL_MAX_DIST / REL_MAX_EXACT)
                             * (REL_BUCKETS - REL_MAX_EXACT)).astype(jnp.int32)
    large = jnp.minimum(large, REL_BUCKETS - 1)
    return jnp.where(n < REL_MAX_EXACT, n, large)


def _bias_tiles(tab, tq, tk, offsets, lo=0, hi=None):
    t = jnp.arange(tq)[:, None]
    c = jnp.arange(tk)[None, :]
    rel = tab.astype(F32) - tab[REL_BUCKETS - 1:].astype(F32)
    tiles = []
    for off in offsets:
        dist = off + t - c
        onehot = (_t5_bucket(dist)[..., None] == jnp.arange(REL_BUCKETS)).astype(F32)
        b = jnp.einsum('tcb,bh->htc', onehot, rel, precision=lax.Precision.HIGHEST)
        ok = dist >= lo
        if hi is not None:
            ok = ok & (dist < hi)
        tiles.append(jnp.where(ok[None], b, NEG))
    return jnp.stack(tiles, axis=1).astype(F32)


def _inproj_kernel(x_ref, g_ref, w_ref, wg_ref, o_ref, go_ref, xn_ref, *, heads_per_tile):
    @pl.when(pl.program_id(1) == 0)
    def _():
        x = x_ref[...]
        y = x * lax.rsqrt(jnp.mean(x * x, axis=-1, keepdims=True) + NORM_EPS)
        xn_ref[...] = (y * g_ref[...]).astype(BF16)
        go_ref[...] = _dot(xn_ref[...], wg_ref[...])

    acc = _dot(xn_ref[...], w_ref[...])
    for hh in range(heads_per_tile):
        o_ref[0, hh] = acc[:, hh * HEAD_DIM:(hh + 1) * HEAD_DIM].astype(BF16)


def _inproj(x2, gain, w, wg, batch, seq):
    n, d = x2.shape
    e = w.shape[1]
    tm, tn = 512, 512
    hpt = tn // HEAD_DIM
    spt = seq // tm
    return pl.pallas_call(
        functools.partial(_inproj_kernel, heads_per_tile=hpt),
        grid=(n // tm, e // tn),
        in_specs=[
            pl.BlockSpec((tm, d), lambda i, j: (i, 0)),
            pl.BlockSpec((1, d), lambda i, j: (0, 0)),
            pl.BlockSpec((d, tn), lambda i, j: (0, j)),
            pl.BlockSpec((d, LANES), lambda i, j: (0, 0)),
        ],
        out_specs=[
            pl.BlockSpec((1, hpt, tm, HEAD_DIM), lambda i, j: (i // spt, j, i % spt, 0)),
            pl.BlockSpec((tm, LANES), lambda i, j: (i, 0)),
        ],
        out_shape=[
            jax.ShapeDtypeStruct((batch, e // HEAD_DIM, seq, HEAD_DIM), BF16),
            jax.ShapeDtypeStruct((n, LANES), F32),
        ],
        scratch_shapes=[pltpu.VMEM((tm, d), BF16)],
        compiler_params=_cparams("parallel", "arbitrary"),
        name="inproj",
    )(x2, gain, w, wg)


def _lane_fold(x, op):
    r = x[:, :LANES]
    for c in range(1, x.shape[1] // LANES):
        r = op(r, x[:, c * LANES:(c + 1) * LANES])
    return r


def _softmax_attend(q, n_tiles, k_tile, v_tile, extra, s_ref, acc_ref, scale):
    rows = q.shape[0]

    def scores(kt, mrun):
        s = _dot_nt(q, k_tile(kt)) * scale + extra(kt)
        s_ref[kt] = s
        return jnp.maximum(mrun, _lane_fold(s, jnp.maximum))

    mrun = lax.fori_loop(0, n_tiles, scores, jnp.full((rows, LANES), NEG, F32))
    m = jnp.broadcast_to(jnp.max(mrun, axis=1, keepdims=True), (rows, LANES))
    acc_ref[...] = jnp.zeros_like(acc_ref)

    def attend(kt, lrun):
        s = s_ref[kt]
        p = jnp.concatenate([jnp.exp(s[:, c * LANES:(c + 1) * LANES] - m) for c in range(s.shape[1] // LANES)],
                            axis=1)
        acc_ref[...] += _dot(p.astype(BF16), v_tile(kt))
        return lrun + _lane_fold(p, jnp.add)

    lrun = lax.fori_loop(0, n_tiles, attend, jnp.zeros((rows, LANES), F32))
    return acc_ref[...] / jnp.sum(lrun, axis=1, keepdims=True)


def _rank_rows(vals):
    row = lax.broadcasted_iota(jnp.int32, vals.shape, 0)
    rank = jnp.zeros(vals.shape, F32)
    for c in range(vals.shape[0]):
        vc = vals[c:c + 1, :]
        tie = jnp.where(row > c, 1.0, 0.0)
        rank = rank + jnp.where(vc > vals, 1.0, jnp.where(vc == vals, tie, 0.0))
    return rank


def _pad_rows_t(x, fill):
    pad = jnp.full((LANES - x.shape[0], x.shape[1]), fill, x.dtype)
    return jnp.concatenate([x, pad], axis=0).T


def _moba_kernel(q_ref, k_ref, v_ref, bt_ref, o_ref, kmean_ref, s_ref, acc_ref, *, scale, nblk):
    i = pl.program_id(2)
    tq = q_ref.shape[2]
    tk = s_ref.shape[2]
    per = tk // tq

    @pl.when(i == 0)
    def _():
        kmean_ref[...] = jnp.zeros_like(kmean_ref)
        kf = k_ref[0, 0].astype(F32).reshape(nblk, tq, HEAD_DIM)
        kmean_ref[0:nblk, :] = jnp.mean(kf, axis=1)

    q = q_ref[0, 0]
    col = lax.broadcasted_iota(jnp.int32, (tq, LANES), 1)
    gate = _dot_nt(kmean_ref[...].astype(BF16), q)[:nblk]
    blk = lax.broadcasted_iota(jnp.int32, gate.shape, 0)
    rank = _rank_rows(jnp.where(blk < i, gate, NEG))
    selneg = _pad_rows_t(jnp.where(((rank < MOBA_TOPK) & (blk < i)) | (blk == i), 0.0, NEG), NEG)

    def extra(kt):
        bias = bt_ref[0, jnp.minimum(i - per * kt, bt_ref.shape[1] - 1)]
        parts = []
        for hb in range(per):
            selcol = jnp.sum(jnp.where(col == per * kt + hb, selneg, 0.0), axis=1, keepdims=True)
            parts.append(bias[:, hb * tq:(hb + 1) * tq] + selcol)
        return jnp.concatenate(parts, axis=1)

    out = _softmax_attend(
        q, i // per + 1,
        lambda kt: k_ref[0, 0, pl.ds(pl.multiple_of(kt * tk, tk), tk), :],
        lambda kt: v_ref[0, 0, pl.ds(pl.multiple_of(kt * tk, tk), tk), :],
        extra, s_ref, acc_ref, scale)
    o_ref[0, 0] = out.astype(o_ref.dtype)


def _moba(proj, bias_tiles):
    b, _, s, _ = proj.shape
    tq = MOBA_BLOCK
    tk = bias_tiles.shape[3]
    nblk = s // tq
    return pl.pallas_call(
        functools.partial(_moba_kernel, scale=HEAD_DIM ** -0.5, nblk=nblk),
        grid=(b, N_MOBA, nblk),
        in_specs=[
            pl.BlockSpec((1, 1, tq, HEAD_DIM), lambda bb, h, i: (bb, _MQ + h, i, 0)),
            pl.BlockSpec((1, 1, s, HEAD_DIM), lambda bb, h, i: (bb, _MK + h, 0, 0)),
            pl.BlockSpec((1, 1, s, HEAD_DIM), lambda bb, h, i: (bb, _MV + h, 0, 0)),
            pl.BlockSpec((1,) + bias_tiles.shape[1:], lambda bb, h, i: (h, 0, 0, 0)),
        ],
        out_specs=pl.BlockSpec((1, 1, tq, HEAD_DIM), lambda bb, h, i: (bb, h, i, 0)),
        out_shape=jax.ShapeDtypeStruct((b, N_MOBA, s, HEAD_DIM), BF16),
        scratch_shapes=[pltpu.VMEM((LANES, HEAD_DIM), F32), pltpu.VMEM((s // tk, tq, tk), F32),
                        pltpu.VMEM((tq, HEAD_DIM), F32)],
        compiler_params=_cparams("parallel", "parallel", "arbitrary"),
        name="moba",
    )(proj, proj, proj, bias_tiles)


def _diff_kernel(q_ref, k_ref, v_ref, bt_ref, lam_ref, gain_ref, o_ref, s_ref, acc_ref, *, scale, lam_init):
    i = pl.program_id(2)
    tq = q_ref.shape[2]
    tk = s_ref.shape[2]
    per = tk // tq
    q = q_ref[0, 0]
    lane = lax.broadcasted_iota(jnp.int32, q.shape, 1)
    half = HEAD_DIM // 2

    def attend(qh):
        return _softmax_attend(
            qh, i // per + 1,
            lambda kt: k_ref[0, 0, pl.ds(pl.multiple_of(kt * tk, tk), tk), :],
            lambda kt: v_ref[0, 0, pl.ds(pl.multiple_of(kt * tk, tk), tk), :],
            lambda kt: bt_ref[0, jnp.minimum(i - per * kt, bt_ref.shape[1] - 1)],
            s_ref, acc_ref, scale)

    o1 = attend(jnp.where(lane < half, q, jnp.zeros_like(q)))
    o2 = attend(jnp.where(lane >= half, q, jnp.zeros_like(q)))

    lp = lam_ref[...]
    lam = (jnp.exp(jnp.sum(lp[0:1] * lp[1:2], axis=1, keepdims=True))
           - jnp.exp(jnp.sum(lp[2:3] * lp[3:4], axis=1, keepdims=True)) + lam_init)
    o = o1 - lam * o2
    y = o * lax.rsqrt(jnp.mean(o * o, axis=-1, keepdims=True) + DIFF_EPS)
    o_ref[0, 0] = ((y * gain_ref[...]) * (1.0 - lam_init)).astype(o_ref.dtype)


def _diff(proj, bias_tiles, lam_params, sub_gain, lam_init):
    b, _, s, _ = proj.shape
    tq = bias_tiles.shape[2]
    tk = bias_tiles.shape[3]
    return pl.pallas_call(
        functools.partial(_diff_kernel, scale=(HEAD_DIM // 2) ** -0.5, lam_init=lam_init),
        grid=(b, N_DIFF, s // tq),
        in_specs=[
            pl.BlockSpec((1, 1, tq, HEAD_DIM), lambda bb, h, i: (bb, _DQ + h, i, 0)),
            pl.BlockSpec((1, 1, s, HEAD_DIM), lambda bb, h, i: (bb, _DK + h, 0, 0)),
            pl.BlockSpec((1, 1, s, HEAD_DIM), lambda bb, h, i: (bb, _DV + h, 0, 0)),
            pl.BlockSpec((1,) + bias_tiles.shape[1:], lambda bb, h, i: (h, 0, 0, 0)),
            pl.BlockSpec(lam_params.shape, lambda bb, h, i: (0, 0)),
            pl.BlockSpec((1, HEAD_DIM), lambda bb, h, i: (0, 0)),
        ],
        out_specs=pl.BlockSpec((1, 1, tq, HEAD_DIM), lambda bb, h, i: (bb, h, i, 0)),
        out_shape=jax.ShapeDtypeStruct((b, N_DIFF, s, HEAD_DIM), BF16),
        scratch_shapes=[pltpu.VMEM((s // tk, tq, tk), F32), pltpu.VMEM((tq, HEAD_DIM), F32)],
        compiler_params=_cparams("parallel", "parallel", "arbitrary"),
        name="diff",
    )(proj, proj, proj, bias_tiles, lam_params, sub_gain)


def _nsa_compress_kernel(x_ref, w_ref, pos_ref, o_ref):
    x = x_ref[0, 0]
    w0, w1 = w_ref[0, 0], w_ref[0, 1]
    y0 = _dot(x, w0)
    y1 = _dot(x, w1)
    pv = _dot(pos_ref[0], w0)[0:1] + _dot(pos_ref[0], w1)[1:2]
    n = y1.shape[0]
    y1_next = pltpu.roll(y1, n - 1, 0)
    o_ref[0, 0] = (y0 + y1_next + pv).astype(o_ref.dtype)


def _nsa_compress(xkv, w, pos):
    b, hh, nrow, wid = xkv.shape
    return pl.pallas_call(
        _nsa_compress_kernel,
        grid=(b, hh),
        in_specs=[
            pl.BlockSpec((1, 1, nrow, wid), lambda bb, h: (bb, h, 0, 0)),
            pl.BlockSpec((1, 2, wid, HEAD_DIM), lambda bb, h: (h // NSA_G, 0, 0, 0)),
            pl.BlockSpec((1, 8, wid), lambda bb, h: (h // NSA_G, 0, 0)),
        ],
        out_specs=pl.BlockSpec((1, 1, nrow, HEAD_DIM), lambda bb, h: (bb, h, 0, 0)),
        out_shape=jax.ShapeDtypeStruct((b, hh, nrow, HEAD_DIM), BF16),
        compiler_params=_cparams("parallel", "parallel"),
        name="nsa_compress",
    )(xkv, w, pos)


def _nsa_cmp_kernel(q_ref, kc_ref, vc_ref, ov_ref, o_ref, sel_ref, *, scale, nsel):
    i = pl.program_id(2)
    tq = q_ref.shape[2]
    ncmp = kc_ref.shape[2]
    kc = kc_ref[0, 0]
    vc = vc_ref[0, 0]
    ov = ov_ref[...]
    rows = NSA_R * tq
    assert tq & (tq - 1) == 0
    pos = i * tq + (lax.broadcasted_iota(jnp.int32, (rows, ncmp), 0) & (tq - 1))
    ncol = lax.broadcasted_iota(jnp.int32, (rows, ncmp), 1)
    ok = (ncol * CMP_STRIDE + (CMP_LEN - 1)) <= pos
    s = jnp.where(ok, _dot_nt(q_ref[0].reshape(rows, HEAD_DIM), kc) * scale, NEG)
    m = jnp.max(s, axis=1, keepdims=True)
    p = jnp.where(ok, jnp.exp(s - m), 0.0)
    l = jnp.sum(p, axis=1, keepdims=True)
    pn = p / jnp.where(l > 0.0, l, 1.0)
    o_ref[0] = _dot(pn.astype(BF16), vc).reshape(NSA_R, tq, HEAD_DIM)
    psum = jnp.sum(pn.reshape(NSA_R, tq, ncmp), axis=0)
    hi = psum.astype(BF16)
    lo = (psum - hi.astype(F32)).astype(BF16)
    imp = _dot(hi, ov) + _dot(lo, ov)

    imp = imp.T[:nsel]
    jj = lax.broadcasted_iota(jnp.int32, imp.shape, 0)
    own = (i * tq + lax.broadcasted_iota(jnp.int32, imp.shape, 1)) // SEL_BLOCK
    forced = (jj == 0) | (jj == own) | (jj == own - 1)
    visible = jj <= own
    rank = _rank_rows(jnp.where(visible, jnp.where(forced, NSA_FORCE, imp), NEG))
    sel_ref[0, 0] = _pad_rows_t(jnp.where((rank < SEL_TOPK) & visible, 0.0, NEG), NEG)


def _nsa_cmp(proj, kvc, overlap):
    b, _, s, _ = proj.shape
    tq = 128
    ncmp = kvc.shape[2]
    return pl.pallas_call(
        functools.partial(_nsa_cmp_kernel, scale=HEAD_DIM ** -0.5, nsel=s // SEL_BLOCK),
        grid=(b, NSA_G, s // tq),
        in_specs=[
            pl.BlockSpec((1, NSA_R, tq, HEAD_DIM), lambda bb, g, i: (bb, _NQ // NSA_R + g, i, 0)),
            pl.BlockSpec((1, 1, ncmp, HEAD_DIM), lambda bb, g, i: (bb, g, 0, 0)),
            pl.BlockSpec((1, 1, ncmp, HEAD_DIM), lambda bb, g, i: (bb, NSA_G + g, 0, 0)),
            pl.BlockSpec(overlap.shape, lambda bb, g, i: (0, 0)),
        ],
        out_specs=[
            pl.BlockSpec((1, NSA_R, tq, HEAD_DIM), lambda bb, g, i: (bb, g, i, 0)),
            pl.BlockSpec((1, 1, tq, LANES), lambda bb, g, i: (bb, g, i, 0)),
        ],
        out_shape=[
            jax.ShapeDtypeStruct((b, N_NSA, s, HEAD_DIM), F32),
            jax.ShapeDtypeStruct((b, NSA_G, s, LANES), F32),
        ],
        compiler_params=_cparams("parallel", "parallel", "parallel"),
        name="nsa_cmp",
    )(proj, kvc, kvc, overlap)


def _nsa_sel_kernel(q_ref, k_ref, v_ref, sel_ref, ex_ref, bt_ref, o_ref, s_ref, acc_ref, *, scale):
    i = pl.program_id(2)
    tq = q_ref.shape[2]
    tk = ex_ref.shape[2]
    rows = NSA_R * tq
    q = q_ref[0].reshape(rows, HEAD_DIM)
    selneg = sel_ref[0, 0].astype(BF16)
    ratio = tk // tq

    def extra(kt):
        mask = _dot(selneg, ex_ref[kt])
        return (bt_ref[0, jnp.minimum(i - ratio * kt, bt_ref.shape[1] - 1)]
                + jnp.concatenate([mask] * NSA_R, axis=0))

    out = _softmax_attend(
        q, i // ratio + 1,
        lambda kt: k_ref[0, 0, pl.ds(pl.multiple_of(kt * tk, tk), tk), :],
        lambda kt: v_ref[0, 0, pl.ds(pl.multiple_of(kt * tk, tk), tk), :],
        extra, s_ref, acc_ref, scale)
    o_ref[0] = out.reshape(NSA_R, tq, HEAD_DIM)


def _nsa_sel(proj, selneg, expand, bias_tiles):
    b, _, s, _ = proj.shape
    tq = 128
    tk = expand.shape[2]
    rows = NSA_R * tq
    return pl.pallas_call(
        functools.partial(_nsa_sel_kernel, scale=HEAD_DIM ** -0.5),
        grid=(b, NSA_G, s // tq),
        in_specs=[
            pl.BlockSpec((1, NSA_R, tq, HEAD_DIM), lambda bb, g, i: (bb, _NQ // NSA_R + g, i, 0)),
            pl.BlockSpec((1, 1, s, HEAD_DIM), lambda bb, g, i: (bb, _NKS + g, 0, 0)),
            pl.BlockSpec((1, 1, s, HEAD_DIM), lambda bb, g, i: (bb, _NVS + g, 0, 0)),
            pl.BlockSpec((1, 1, tq, LANES), lambda bb, g, i: (bb, g, i, 0)),
            pl.BlockSpec(expand.shape, lambda bb, g, i: (0, 0, 0)),
            pl.BlockSpec((1,) + bias_tiles.shape[1:], lambda bb, g, i: (g, 0, 0, 0)),
        ],
        out_specs=pl.BlockSpec((1, NSA_R, tq, HEAD_DIM), lambda bb, g, i: (bb, g, i, 0)),
        out_shape=jax.ShapeDtypeStruct((b, N_NSA, s, HEAD_DIM), F32),
        scratch_shapes=[pltpu.VMEM((s // tk, rows, tk), F32), pltpu.VMEM((rows, HEAD_DIM), F32)],
        compiler_params=_cparams("parallel", "parallel", "arbitrary"),
        name="nsa_sel",
    )(proj, proj, proj, selneg, expand, bias_tiles)


def _nsa_win_kernel(q_ref, k_ref, v_ref, bt_ref, oc_ref, os_ref, gate_ref, o_ref, *, scale, nband):
    i = pl.program_id(2)
    tq = q_ref.shape[2]
    rows = NSA_R * tq
    q = q_ref[0].reshape(rows, HEAD_DIM)
    scores, values = [], []
    for m in range(nband):
        kt = i - m
        start = pl.multiple_of(jnp.maximum(kt, 0) * tq, tq)
        s = _dot_nt(q, k_ref[0, 0, pl.ds(start, tq), :]) * scale + bt_ref[0, m]
        scores.append(jnp.where(kt >= 0, s, NEG))
        values.append(v_ref[0, 0, pl.ds(start, tq), :])
    mx = scores[0]
    for s in scores[1:]:
        mx = jnp.maximum(mx, s)
    mx = jnp.broadcast_to(jnp.max(mx, axis=1, keepdims=True), (rows, LANES))
    lsum = jnp.zeros((rows, LANES), F32)
    acc = jnp.zeros((rows, HEAD_DIM), F32)
    for s, v in zip(scores, values):
        p = jnp.exp(s - mx)
        lsum = lsum + p
        acc = acc + _dot(p.astype(BF16), v)
    ow = (acc / jnp.sum(lsum, axis=1, keepdims=True)).reshape(NSA_R, tq, HEAD_DIM)
    g = jax.nn.sigmoid(gate_ref[0, 0])
    for r in range(NSA_R):
        gc, gs, gw = (g[:, 3 * r + br:3 * r + br + 1] for br in range(3))
        o_ref[0, r] = (gc * oc_ref[0, r] + gs * os_ref[0, r] + gw * ow[r]).astype(o_ref.dtype)


def _nsa_win_combine(proj, bias_tiles, o_cmp, o_sel, gates3):
    b, _, s, _ = proj.shape
    tq = 128
    rows = NSA_R * tq
    nband = bias_tiles.shape[1]
    blk = pl.BlockSpec((1, NSA_R, tq, HEAD_DIM), lambda bb, g, i: (bb, g, i, 0))
    return pl.pallas_call(
        functools.partial(_nsa_win_kernel, scale=HEAD_DIM ** -0.5, nband=nband),
        grid=(b, NSA_G, s // tq),
        in_specs=[
            pl.BlockSpec((1, NSA_R, tq, HEAD_DIM), lambda bb, g, i: (bb, _NQ // NSA_R + g, i, 0)),
            pl.BlockSpec((1, 1, s, HEAD_DIM), lambda bb, g, i: (bb, _NKW + g, 0, 0)),
            pl.BlockSpec((1, 1, s, HEAD_DIM), lambda bb, g, i: (bb, _NVW + g, 0, 0)),
            pl.BlockSpec((1, nband, rows, tq), lambda bb, g, i: (g, 0, 0, 0)),
            blk, blk,
            pl.BlockSpec((1, 1, tq, LANES), lambda bb, g, i: (bb, g, i, 0)),
        ],
        out_specs=blk,
        out_shape=jax.ShapeDtypeStruct((b, N_NSA, s, HEAD_DIM), BF16),
        compiler_params=_cparams("parallel", "parallel", "parallel"),
        name="nsa_win",
    )(proj, proj, proj, bias_tiles, o_cmp, o_sel, gates3)


def _outproj_kernel(om_ref, od_ref, on_ref, w_ref, x_ref, o_ref, cat_ref):
    h = 0
    for ref in (om_ref, od_ref, on_ref):
        for hh in range(ref.shape[1]):
            cat_ref[:, h * HEAD_DIM:(h + 1) * HEAD_DIM] = ref[0, hh]
            h += 1
    o_ref[...] = x_ref[...] + _dot(cat_ref[...], w_ref[...])


def _outproj(o_moba, o_diff, o_nsa, w, x2):
    n, d = x2.shape
    b, _, s, _ = o_moba.shape
    tm = 512
    spt = s // tm

    def hspec(nh):
        return pl.BlockSpec((1, nh, tm, HEAD_DIM), lambda i: (i // spt, 0, i % spt, 0))

    return pl.pallas_call(
        _outproj_kernel,
        grid=(n // tm,),
        in_specs=[hspec(N_MOBA), hspec(N_DIFF), hspec(N_NSA),
                  pl.BlockSpec(w.shape, lambda i: (0, 0)),
                  pl.BlockSpec((tm, d), lambda i: (i, 0))],
        out_specs=pl.BlockSpec((tm, d), lambda i: (i, 0)),
        out_shape=jax.ShapeDtypeStruct((n, d), F32),
        scratch_shapes=[pltpu.VMEM((tm, w.shape[0]), BF16)],
        compiler_params=_cparams("parallel"),
        name="outproj",
    )(o_moba, o_diff, o_nsa, w, x2)


def _peer_query_kernel(x_ref, g_ref, wq_ref, sk_ref, sc_ref, xt_ref):
    x = x_ref[...]
    y = x * lax.rsqrt(jnp.mean(x * x, axis=-1, keepdims=True) + NORM_EPS)
    xn = y * g_ref[...]
    xt_ref[...] = xn.T.astype(BF16)
    qry = _dot(xn.astype(BF16), wq_ref[...]).astype(BF16)
    for hp in range(sk_ref.shape[0]):
        sc_ref[hp] = _dot_nt(sk_ref[hp], qry[:, hp * PEER_NKEYS:(hp + 1) * PEER_NKEYS])


def _peer_query(x2, gain, wq, subkeys):
    n, d = x2.shape
    tm = 512
    nhp = subkeys.shape[0]
    return pl.pallas_call(
        _peer_query_kernel,
        grid=(n // tm,),
        in_specs=[
            pl.BlockSpec((tm, d), lambda i: (i, 0)),
            pl.BlockSpec((1, d), lambda i: (0, 0)),
            pl.BlockSpec(wq.shape, lambda i: (0, 0)),
            pl.BlockSpec(subkeys.shape, lambda i: (0, 0, 0)),
        ],
        out_specs=[
            pl.BlockSpec((nhp, PEER_NKEYS, tm), lambda i: (0, 0, i)),
            pl.BlockSpec((d, tm), lambda i: (0, i)),
        ],
        out_shape=[
            jax.ShapeDtypeStruct((nhp, PEER_NKEYS, n), F32),
            jax.ShapeDtypeStruct((d, n), BF16),
        ],
        compiler_params=_cparams("parallel"),
        name="peer_query",
    )(x2, gain, wq, subkeys)


def _top_values(a, k, exact_ties):
    rowid = lax.broadcasted_iota(jnp.int32, a.shape, 0)
    spare = jnp.sum(jnp.where(a == NEG, 1.0, 0.0), axis=0, keepdims=True)
    start = a
    member = jnp.zeros(a.shape, F32)
    vals = []
    for _ in range(k):
        m = jnp.max(a, axis=0, keepdims=True)
        hit = a == m
        if exact_ties:
            first = jnp.min(jnp.where(hit, rowid, a.shape[0]), axis=0, keepdims=True)
            hit = rowid == first
            member = jnp.where(hit, 1.0, member)
        a = jnp.where(hit, NEG, a)
        vals.append(m)
    if not exact_ties:
        member = jnp.where(start >= vals[-1], 1.0, 0.0)
    removed = jnp.sum(jnp.where(a == NEG, 1.0, 0.0), axis=0, keepdims=True) - spare
    return vals, member, jnp.where(removed > k, 1.0, 0.0)


def _route_head(a, b, exact_ties):
    k = PEER_TOPK
    av, am, fa = _top_values(a, k, exact_ties)
    bv, bm, fb = _top_values(b, k, exact_ties)
    pairs = [(r, c) for r in range(k) for c in range(k) if (r + 1) * (c + 1) <= k + 1]
    nrow = -(-len(pairs) // 8) * 8
    crow = lax.broadcasted_iota(jnp.int32, (nrow, a.shape[1]), 0)
    cmat = jnp.full((nrow, a.shape[1]), NEG, F32)
    for idx, (r, c) in enumerate(pairs):
        cmat = jnp.where(crow == idx, av[r] + bv[c], cmat)
    top, _, fc = _top_values(cmat, k + 1, exact_ties)
    peak = top[0]
    z = jnp.zeros_like(peak)
    for t in top[:k]:
        z = z + jnp.exp(t - peak)
    mid = 0.5 * (top[k - 1] + top[k])
    w1 = jnp.where(am > 0.0, jnp.exp(a - av[0]), 0.0)
    w2 = jnp.where(bm > 0.0, jnp.exp(b - bv[0]), 0.0) / z
    return w1, w2, jnp.exp(mid - peak) / z, jnp.maximum(jnp.maximum(fa, fb), fc)


def _peer_route_kernel(sc_ref, w1_ref, w2_ref, th_ref):
    def route(exact_ties):
        flag = None
        for h in range(PEER_HEADS):
            w1, w2, th, f = _route_head(sc_ref[2 * h], sc_ref[2 * h + 1], exact_ties)
            w1_ref[h] = w1
            w2_ref[h] = w2
            th_ref[h:h + 1, :] = th
            flag = f if flag is None else jnp.maximum(flag, f)
        return flag

    flag = route(False)

    @pl.when(jnp.max(flag) > 0.0)
    def _():
        route(True)


def _peer_route(sc):
    nhp, nk, n = sc.shape
    t = 256
    return pl.pallas_call(
        _peer_route_kernel,
        grid=(n // t,),
        in_specs=[pl.BlockSpec((nhp, nk, t), lambda i: (0, 0, i))],
        out_specs=[
            pl.BlockSpec((PEER_HEADS, nk, t), lambda i: (0, 0, i)),
            pl.BlockSpec((PEER_HEADS, nk, t), lambda i: (0, 0, i)),
            pl.BlockSpec((PEER_HEADS, t), lambda i: (0, i)),
        ],
        out_shape=[
            jax.ShapeDtypeStruct((PEER_HEADS, nk, n), F32),
            jax.ShapeDtypeStruct((PEER_HEADS, nk, n), F32),
            jax.ShapeDtypeStruct((PEER_HEADS, n), F32),
        ],
        compiler_params=_cparams("parallel"),
        name="peer_route",
    )(sc)


def _peer_expert_kernel(xt_ref, u_ref, vt_ref, w1_ref, w2_ref, th_ref, o_ref, a_ref, *, rows_per_tile, n_tiles):
    e = pl.program_id(1)

    @pl.when(e == 0)
    def _():
        o_ref[...] = jnp.zeros_like(o_ref)
        a_ref[...] = jnp.zeros_like(a_ref)

    slot = e % 2
    row0 = jnp.minimum(e, n_tiles - 1) * rows_per_tile
    a_prev = a_ref[1 - slot]
    orows = o_ref.shape[0] // rows_per_tile
    for ii in range(rows_per_tile):
        hv = _dot(u_ref[ii * PEER_NKEYS:(ii + 1) * PEER_NKEYS, :], xt_ref[...])
        o_ref[ii * orows:(ii + 1) * orows, :] += _dot(vt_ref[ii * orows:(ii + 1) * orows, :], a_prev)
        wrows = [w1_ref[hd, pl.ds(row0 + ii, 1), :] for hd in range(PEER_HEADS)]
        for lt in range(xt_ref.shape[1] // LANES):
            ls = slice(lt * LANES, (lt + 1) * LANES)
            g = None
            for hd in range(PEER_HEADS):
                p = wrows[hd][:, ls] * w2_ref[hd, :, ls]
                term = jnp.where(p >= th_ref[hd:hd + 1, ls], p, 0.0)
                g = term if g is None else g + term
            hl = hv[:, ls]
            gelu = 0.5 * hl * (1.0 + lax.erf(hl * (2.0 ** -0.5)))
            a_ref[slot, ii * PEER_NKEYS:(ii + 1) * PEER_NKEYS, ls] = (gelu * g).astype(BF16)


def _peer_experts(xt, u, vt, w1, w2, th):
    d, n = xt.shape
    ne = u.shape[0]
    t = min(512, n)
    rpt = 4
    et = rpt * PEER_NKEYS
    n_tiles = ne // et
    return pl.pallas_call(
        functools.partial(_peer_expert_kernel, rows_per_tile=rpt, n_tiles=n_tiles),
        grid=(n // t, n_tiles + 1),
        in_specs=[
            pl.BlockSpec((d, t), lambda i, e: (0, i)),
            pl.BlockSpec((et, d), lambda i, e: (jnp.minimum(e, n_tiles - 1), 0)),
            pl.BlockSpec((d, et), lambda i, e: (0, jnp.maximum(e - 1, 0))),
            pl.BlockSpec((PEER_HEADS, PEER_NKEYS, t), lambda i, e: (0, 0, i)),
            pl.BlockSpec((PEER_HEADS, PEER_NKEYS, t), lambda i, e: (0, 0, i)),
            pl.BlockSpec((PEER_HEADS, t), lambda i, e: (0, i)),
        ],
        out_specs=pl.BlockSpec((d, t), lambda i, e: (0, i)),
        out_shape=jax.ShapeDtypeStruct((d, n), F32),
        scratch_shapes=[pltpu.VMEM((2, et, t), BF16)],
        compiler_params=_cparams("parallel", "arbitrary"),
        name="peer_experts",
    )(xt, u, vt, w1, w2, th)


def _residual_t_kernel(x_ref, yt_ref, o_ref):
    o_ref[...] = x_ref[...] + yt_ref[...].T


def _residual_t(x2, yt):
    n, d = x2.shape
    tm = 512
    return pl.pallas_call(
        _residual_t_kernel,
        grid=(n // tm,),
        in_specs=[pl.BlockSpec((tm, d), lambda i: (i, 0)), pl.BlockSpec((d, tm), lambda i: (0, i))],
        out_specs=pl.BlockSpec((tm, d), lambda i: (i, 0)),
        out_shape=jax.ShapeDtypeStruct((n, d), F32),
        compiler_params=_cparams("parallel"),
        name="residual_t",
    )(x2, yt)


def _final_norm_kernel(x_ref, g_ref, o_ref):
    x = x_ref[...]
    y = x * lax.rsqrt(jnp.mean(x * x, axis=-1, keepdims=True) + NORM_EPS)
    o_ref[...] = y * g_ref[...]


def _final_norm(x2, gain):
    n, d = x2.shape
    tm = 512
    return pl.pallas_call(
        _final_norm_kernel,
        grid=(n // tm,),
        in_specs=[pl.BlockSpec((tm, d), lambda i: (i, 0)), pl.BlockSpec((1, d), lambda i: (0, 0))],
        out_specs=pl.BlockSpec((tm, d), lambda i: (i, 0)),
        out_shape=jax.ShapeDtypeStruct((n, d), F32),
        compiler_params=_cparams("parallel"),
        name="final_norm",
    )(x2, gain)


def _position_tables(rel_bias, seq):
    bias_moba = rel_bias[:, :N_MOBA]
    bias_diff = rel_bias[:, N_MOBA:N_MOBA + N_DIFF]
    bias_nsa = rel_bias[:, N_MOBA + N_DIFF:]
    blk = MOBA_BLOCK
    offs = tuple(blk * m for m in range(FLASH_TK // blk + 2))
    tabs = {
        "moba": _bias_tiles(bias_moba, blk, FLASH_TK, offs),
        "diff": _bias_tiles(bias_diff, blk, FLASH_TK, offs),
    }
    tq, tk = NSA_TQ, NSA_SEL_TK
    n_off = tk // tq + 2
    bt_sel = _bias_tiles(bias_nsa, tq, tk, tuple(tq * m for m in range(n_off)))
    tabs["sel"] = bt_sel.reshape(NSA_G, NSA_R, n_off, tq, tk).transpose(0, 2, 1, 3, 4).reshape(
        NSA_G, n_off, NSA_R * tq, tk)
    nband = NSA_WINDOW // tq + 1
    bt_win = _bias_tiles(bias_nsa, tq, tq, tuple(tq * m for m in range(nband)), lo=0, hi=NSA_WINDOW)
    tabs["win"] = bt_win.reshape(NSA_G, NSA_R, nband, tq, tq).transpose(0, 2, 1, 3, 4).reshape(
        NSA_G, nband, NSA_R * tq, tq)

    nrow = seq // CMP_STRIDE
    n_sel = seq // SEL_BLOCK
    cmp_start = jnp.arange(nrow) * CMP_STRIDE
    sel_start = jnp.arange(LANES) * SEL_BLOCK
    tabs["overlap"] = ((cmp_start[:, None] <= (sel_start + SEL_BLOCK - 1)[None, :])
                       & ((cmp_start + CMP_LEN - 1)[:, None] >= sel_start[None, :])
                       & (jnp.arange(nrow) < (seq - CMP_LEN) // CMP_STRIDE + 1)[:, None]
                       & (jnp.arange(LANES) < n_sel)[None, :]).astype(BF16)
    expand = (jnp.arange(LANES)[:, None] == (jnp.arange(seq) // SEL_BLOCK)[None, :]).astype(BF16)
    tabs["expand"] = expand.reshape(LANES, seq // tk, tk).transpose(1, 0, 2)
    return tabs


def _attention_block(x2, batch, seq, tabs, attn_norm, w_in, diff_lambda, diff_subln, cmp_pos, cmp_w, w_out,
                     lam_init):
    d = x2.shape[1]
    n_main = _N_PROJ_HEADS * HEAD_DIM
    w_main = w_in[:, :n_main].astype(BF16)
    w_gate = jnp.pad(w_in[:, n_main:], ((0, 0), (0, LANES - (w_in.shape[1] - n_main)))).astype(BF16)
    proj, gates = _inproj(x2, attn_norm.reshape(1, d), w_main, w_gate, batch, seq)

    o_moba = _moba(proj, tabs["moba"])
    o_diff = _diff(proj, tabs["diff"], diff_lambda, diff_subln.reshape(1, HEAD_DIM), lam_init)

    nrow = seq // CMP_STRIDE
    wid = CMP_STRIDE * HEAD_DIM
    xkv = proj[:, _NKC:_NKC + 2 * NSA_G].reshape(batch, 2 * NSA_G, nrow, wid)
    w_cmp = cmp_w.reshape(2, 2, wid, HEAD_DIM).astype(BF16)
    pos = jnp.pad(cmp_pos.reshape(2, 2, wid), ((0, 0), (0, 6), (0, 0))).astype(BF16)
    kvc = _nsa_compress(xkv, w_cmp, pos)

    o_cmp, selneg = _nsa_cmp(proj, kvc, tabs["overlap"])
    o_sel = _nsa_sel(proj, selneg, tabs["expand"], tabs["sel"])

    n_gate = N_NSA * 3
    gates_g = gates[:, :n_gate].reshape(batch, seq, NSA_G, n_gate // NSA_G).transpose(0, 2, 1, 3)
    gates_g = jnp.pad(gates_g, ((0, 0), (0, 0), (0, 0), (0, LANES - n_gate // NSA_G)))
    o_nsa = _nsa_win_combine(proj, tabs["win"], o_cmp, o_sel, gates_g)

    return _outproj(o_moba, o_diff, o_nsa, w_out.astype(BF16), x2)


def _peer_block(x2, ffn_norm, wq, subkeys, u_tab, v_tab):
    d = x2.shape[1]
    sk = subkeys.reshape(PEER_HEADS * 2, PEER_NKEYS, subkeys.shape[-1]).astype(BF16)
    sc, xt = _peer_query(x2, ffn_norm.reshape(1, d), wq.astype(BF16), sk)
    w1, w2, th = _peer_route(sc)
    yt = _peer_experts(xt, u_tab.astype(BF16), v_tab.astype(BF16).T, w1, w2, th)
    return _residual_t(x2, yt)


def kernel(x, rel_bias, attn_norm, w_in, diff_lambda, diff_subln, nsa_cmp_pos, nsa_cmp_w, w_out, ffn_norm,
           peer_wq, peer_subkeys, peer_u, peer_v, final_norm):
    batch, seq, d = x.shape
    x2 = x.reshape(batch * seq, d)
    tabs = _position_tables(rel_bias, seq)
    for l in range(attn_norm.shape[0]):
        lam_init = 0.8 - 0.6 * math.exp(-0.3 * l)
        x2 = _attention_block(x2, batch, seq, tabs, attn_norm[l], w_in[l], diff_lambda[l], diff_subln[l],
                              nsa_cmp_pos[l], nsa_cmp_w[l], w_out[l], lam_init)
        x2 = _peer_block(x2, ffn_norm[l], peer_wq[l], peer_subkeys[l], peer_u[l], peer_v[l])
    return _final_norm(x2, final_norm.reshape(1, d)).reshape(batch, seq, d)
```

```python
import functools
import math

import jax
import jax.numpy as jnp
from jax import lax
from jax.experimental import pallas as pl
from jax.experimental.pallas import tpu as pltpu

F32 = jnp.float32
BF16 = jnp.bfloat16
NEG = -1e30

HEAD_DIM = 128
LANES = 128
VMEM_LIMIT = 56 * 1024 * 1024

N_MOBA = 4
N_DIFF = 4
N_NSA = 8
NSA_G = 2
NSA_R = 4
MOBA_BLOCK = 256
MOBA_TOPK = 3
DIFF_EPS = 1e-5
NORM_EPS = 1e-6
CMP_LEN = 32
CMP_STRIDE = 16
SEL_BLOCK = 64
SEL_TOPK = 16
NSA_WINDOW = 512
NSA_FORCE = 1e6
FLASH_TK = 1024
NSA_TQ = 128
NSA_SEL_TK = 512
REL_BUCKETS = 32
REL_MAX_EXACT = 16
REL_MAX_DIST = 128
PEER_HEADS = 8
PEER_NKEYS = 128
PEER_TOPK = 16

_MQ, _MK, _MV = 0, 4, 8
_DQ, _DK, _DV = 12, 16, 20
_NQ = 24
_NKC, _NVC, _NKS, _NVS, _NKW, _NVW = 32, 34, 36, 38, 40, 42
_N_PROJ_HEADS = 44

_NT = (((1,), (1,)), ((), ()))


def _cparams(*sem):
    return pltpu.CompilerParams(dimension_semantics=sem, vmem_limit_bytes=VMEM_LIMIT)


def _dot(a, b):
    return jnp.dot(a, b, preferred_element_type=F32)


def _dot_nt(a, b):
    return lax.dot_general(a, b, _NT, preferred_element_type=F32)


def _t5_bucket(dist):
    n = jnp.maximum(dist, 0)
    nf = jnp.maximum(n, REL_MAX_EXACT).astype(F32)
    large = REL_MAX_EXACT + (jnp.log(nf / REL_MAX_EXACT) / math.log(REL_MAX_DIST / REL_MAX_EXACT)
                             * (REL_BUCKETS - REL_MAX_EXACT)).astype(jnp.int32)
    large = jnp.minimum(large, REL_BUCKETS - 1)
    return jnp.where(n < REL_MAX_EXACT, n, large)


def _bias_tiles(tab, tq, tk, offsets, lo=0, hi=None, scale=1.0):
    t = jnp.arange(tq)[:, None]
    c = jnp.arange(tk)[None, :]
    rel = tab.astype(F32) - tab[REL_BUCKETS - 1:].astype(F32)
    tiles = []
    for off in offsets:
        dist = off + t - c
        onehot = (_t5_bucket(dist)[..., None] == jnp.arange(REL_BUCKETS)).astype(F32)
        b = jnp.einsum('tcb,bh->htc', onehot, rel, precision=lax.Precision.HIGHEST)
        ok = dist >= lo
        if hi is not None:
            ok = ok & (dist < hi)
        tiles.append(jnp.where(ok[None], b / scale, NEG))
    return jnp.stack(tiles, axis=1).astype(F32)


def _inproj_kernel(x_ref, g_ref, w_ref, wg_ref, o_ref, go_ref, xn_ref, *, heads_per_tile):
    @pl.when(pl.program_id(1) == 0)
    def _():
        x = x_ref[...]
        y = x * lax.rsqrt(jnp.mean(x * x, axis=-1, keepdims=True) + NORM_EPS)
        xn_ref[...] = (y * g_ref[...]).astype(BF16)
        go_ref[...] = _dot(xn_ref[...], wg_ref[...])

    acc = _dot(xn_ref[...], w_ref[...])
    for hh in range(heads_per_tile):
        o_ref[0, hh] = acc[:, hh * HEAD_DIM:(hh + 1) * HEAD_DIM].astype(BF16)


def _inproj(x2, gain, w, wg, batch, seq):
    n, d = x2.shape
    e = w.shape[1]
    tm, tn = min(1024, seq), 512
    hpt = tn // HEAD_DIM
    spt = seq // tm
    return pl.pallas_call(
        functools.partial(_inproj_kernel, heads_per_tile=hpt),
        grid=(n // tm, e // tn),
        in_specs=[
            pl.BlockSpec((tm, d), lambda i, j: (i, 0)),
            pl.BlockSpec((1, d), lambda i, j: (0, 0)),
            pl.BlockSpec((d, tn), lambda i, j: (0, j)),
            pl.BlockSpec((d, LANES), lambda i, j: (0, 0)),
        ],
        out_specs=[
            pl.BlockSpec((1, hpt, tm, HEAD_DIM), lambda i, j: (i // spt, j, i % spt, 0)),
            pl.BlockSpec((tm, LANES), lambda i, j: (i, 0)),
        ],
        out_shape=[
            jax.ShapeDtypeStruct((batch, e // HEAD_DIM, seq, HEAD_DIM), BF16),
            jax.ShapeDtypeStruct((n, LANES), F32),
        ],
        scratch_shapes=[pltpu.VMEM((tm, d), BF16)],
        compiler_params=_cparams("parallel", "arbitrary"),
        name="inproj",
    )(x2, gain, w, wg)


def _lane_fold(x, op):
    r = x[:, :LANES]
    for c in range(1, x.shape[1] // LANES):
        r = op(r, x[:, c * LANES:(c + 1) * LANES])
    return r


def _softmax_attend(q, n_tiles, k_tile, v_tile, extra, s_ref, acc_ref, scale):
    rows = q.shape[0]
    log2e_scale = scale * math.log2(math.e)

    def scores(kt, mrun):
        s = _dot_nt(q, k_tile(kt)) + extra(kt)
        s_ref[kt] = s
        return jnp.maximum(mrun, _lane_fold(s, jnp.maximum))

    mrun = lax.fori_loop(0, n_tiles, scores, jnp.full((rows, LANES), NEG, F32))
    m = jnp.broadcast_to(jnp.max(mrun, axis=1, keepdims=True), (rows, LANES))
    acc_ref[...] = jnp.zeros_like(acc_ref)

    def attend(kt, lrun):
        s = s_ref[kt]
        p = jnp.concatenate([jnp.exp2((s[:, c * LANES:(c + 1) * LANES] - m) * log2e_scale)
                             for c in range(s.shape[1] // LANES)], axis=1)
        acc_ref[...] += _dot(p.astype(BF16), v_tile(kt))
        return lrun + _lane_fold(p, jnp.add)

    lrun = lax.fori_loop(0, n_tiles, attend, jnp.zeros((rows, LANES), F32))
    return acc_ref[...] / jnp.sum(lrun, axis=1, keepdims=True)


def _rank_rows(vals):
    row = lax.broadcasted_iota(jnp.int32, vals.shape, 0)
    rank = jnp.zeros(vals.shape, F32)
    for c in range(vals.shape[0]):
        vc = vals[c:c + 1, :]
        tie = jnp.where(row > c, 1.0, 0.0)
        rank = rank + jnp.where(vc > vals, 1.0, jnp.where(vc == vals, tie, 0.0))
    return rank


def _pad_rows_t(x, fill):
    pad = jnp.full((LANES - x.shape[0], x.shape[1]), fill, x.dtype)
    return jnp.concatenate([x, pad], axis=0).T


def _moba_kernel(q_ref, k_ref, v_ref, bt_ref, o_ref, kmean_ref, s_ref, acc_ref, *, scale, nblk):
    i = pl.program_id(2)
    tq = q_ref.shape[2]
    tk = s_ref.shape[2]
    per = tk // tq

    @pl.when(i == 0)
    def _():
        kmean_ref[...] = jnp.zeros_like(kmean_ref)
        kf = k_ref[0, 0].astype(F32).reshape(nblk, tq, HEAD_DIM)
        kmean_ref[0:nblk, :] = jnp.mean(kf, axis=1)

    q = q_ref[0, 0]
    col = lax.broadcasted_iota(jnp.int32, (tq, LANES), 1)
    gate = _dot_nt(kmean_ref[...].astype(BF16), q)[:nblk]
    blk = lax.broadcasted_iota(jnp.int32, gate.shape, 0)
    rank = _rank_rows(jnp.where(blk < i, gate, NEG))
    selneg = _pad_rows_t(jnp.where(((rank < MOBA_TOPK) & (blk < i)) | (blk == i), 0.0, NEG), NEG)

    def extra(kt):
        bias = bt_ref[0, jnp.minimum(i - per * kt, bt_ref.shape[1] - 1)]
        parts = []
        for hb in range(per):
            selcol = jnp.sum(jnp.where(col == per * kt + hb, selneg, 0.0), axis=1, keepdims=True)
            parts.append(bias[:, hb * tq:(hb + 1) * tq] + selcol)
        return jnp.concatenate(parts, axis=1)

    out = _softmax_attend(
        q, i // per + 1,
        lambda kt: k_ref[0, 0, pl.ds(pl.multiple_of(kt * tk, tk), tk), :],
        lambda kt: v_ref[0, 0, pl.ds(pl.multiple_of(kt * tk, tk), tk), :],
        extra, s_ref, acc_ref, scale)
    o_ref[0, 0] = out.astype(o_ref.dtype)


def _moba(proj, bias_tiles):
    b, _, s, _ = proj.shape
    tq = MOBA_BLOCK
    tk = bias_tiles.shape[3]
    nblk = s // tq
    return pl.pallas_call(
        functools.partial(_moba_kernel, scale=HEAD_DIM ** -0.5, nblk=nblk),
        grid=(b, N_MOBA, nblk),
        in_specs=[
            pl.BlockSpec((1, 1, tq, HEAD_DIM), lambda bb, h, i: (bb, _MQ + h, i, 0)),
            pl.BlockSpec((1, 1, s, HEAD_DIM), lambda bb, h, i: (bb, _MK + h, 0, 0)),
            pl.BlockSpec((1, 1, s, HEAD_DIM), lambda bb, h, i: (bb, _MV + h, 0, 0)),
            pl.BlockSpec((1,) + bias_tiles.shape[1:], lambda bb, h, i: (h, 0, 0, 0)),
        ],
        out_specs=pl.BlockSpec((1, 1, tq, HEAD_DIM), lambda bb, h, i: (bb, h, i, 0)),
        out_shape=jax.ShapeDtypeStruct((b, N_MOBA, s, HEAD_DIM), BF16),
        scratch_shapes=[pltpu.VMEM((LANES, HEAD_DIM), F32), pltpu.VMEM((s // tk, tq, tk), F32),
                        pltpu.VMEM((tq, HEAD_DIM), F32)],
        compiler_params=_cparams("parallel", "parallel", "arbitrary"),
        name="moba",
    )(proj, proj, proj, bias_tiles)


def _diff_kernel(q_ref, k_ref, v_ref, bt_ref, lam_ref, gain_ref, o_ref, s_ref, acc_ref, *, scale, lam_init):
    i = pl.program_id(2)
    tq = q_ref.shape[2]
    tk = s_ref.shape[2]
    per = tk // tq
    q = q_ref[0, 0]
    lane = lax.broadcasted_iota(jnp.int32, q.shape, 1)
    half = HEAD_DIM // 2

    def attend(qh):
        return _softmax_attend(
            qh, i // per + 1,
            lambda kt: k_ref[0, 0, pl.ds(pl.multiple_of(kt * tk, tk), tk), :],
            lambda kt: v_ref[0, 0, pl.ds(pl.multiple_of(kt * tk, tk), tk), :],
            lambda kt: bt_ref[0, jnp.minimum(i - per * kt, bt_ref.shape[1] - 1)],
            s_ref, acc_ref, scale)

    o1 = attend(jnp.where(lane < half, q, jnp.zeros_like(q)))
    o2 = attend(jnp.where(lane >= half, q, jnp.zeros_like(q)))

    lp = lam_ref[...]
    lam = (jnp.exp(jnp.sum(lp[0:1] * lp[1:2], axis=1, keepdims=True))
           - jnp.exp(jnp.sum(lp[2:3] * lp[3:4], axis=1, keepdims=True)) + lam_init)
    o = o1 - lam * o2
    y = o * lax.rsqrt(jnp.mean(o * o, axis=-1, keepdims=True) + DIFF_EPS)
    o_ref[0, 0] = ((y * gain_ref[...]) * (1.0 - lam_init)).astype(o_ref.dtype)


def _diff(proj, bias_tiles, lam_params, sub_gain, lam_init):
    b, _, s, _ = proj.shape
    tq = bias_tiles.shape[2]
    tk = bias_tiles.shape[3]
    return pl.pallas_call(
        functools.partial(_diff_kernel, scale=(HEAD_DIM // 2) ** -0.5, lam_init=lam_init),
        grid=(b, N_DIFF, s // tq),
        in_specs=[
            pl.BlockSpec((1, 1, tq, HEAD_DIM), lambda bb, h, i: (bb, _DQ + h, i, 0)),
            pl.BlockSpec((1, 1, s, HEAD_DIM), lambda bb, h, i: (bb, _DK + h, 0, 0)),
            pl.BlockSpec((1, 1, s, HEAD_DIM), lambda bb, h, i: (bb, _DV + h, 0, 0)),
            pl.BlockSpec((1,) + bias_tiles.shape[1:], lambda bb, h, i: (h, 0, 0, 0)),
            pl.BlockSpec(lam_params.shape, lambda bb, h, i: (0, 0)),
            pl.BlockSpec((1, HEAD_DIM), lambda bb, h, i: (0, 0)),
        ],
        out_specs=pl.BlockSpec((1, 1, tq, HEAD_DIM), lambda bb, h, i: (bb, h, i, 0)),
        out_shape=jax.ShapeDtypeStruct((b, N_DIFF, s, HEAD_DIM), BF16),
        scratch_shapes=[pltpu.VMEM((s // tk, tq, tk), F32), pltpu.VMEM((tq, HEAD_DIM), F32)],
        compiler_params=_cparams("parallel", "parallel", "arbitrary"),
        name="diff",
    )(proj, proj, proj, bias_tiles, lam_params, sub_gain)


def _nsa_compress_kernel(x_ref, w_ref, pos_ref, o_ref):
    x = x_ref[0, 0]
    w0, w1 = w_ref[0, 0], w_ref[0, 1]
    y0 = _dot(x, w0)
    y1 = _dot(x, w1)
    pv = _dot(pos_ref[0], w0)[0:1] + _dot(pos_ref[0], w1)[1:2]
    n = y1.shape[0]
    y1_next = pltpu.roll(y1, n - 1, 0)
    o_ref[0, 0] = (y0 + y1_next + pv).astype(o_ref.dtype)


def _nsa_compress(xkv, w, pos):
    b, hh, nrow, wid = xkv.shape
    return pl.pallas_call(
        _nsa_compress_kernel,
        grid=(b, hh),
        in_specs=[
            pl.BlockSpec((1, 1, nrow, wid), lambda bb, h: (bb, h, 0, 0)),
            pl.BlockSpec((1, 2, wid, HEAD_DIM), lambda bb, h: (h // NSA_G, 0, 0, 0)),
            pl.BlockSpec((1, 8, wid), lambda bb, h: (h // NSA_G, 0, 0)),
        ],
        out_specs=pl.BlockSpec((1, 1, nrow, HEAD_DIM), lambda bb, h: (bb, h, 0, 0)),
        out_shape=jax.ShapeDtypeStruct((b, hh, nrow, HEAD_DIM), BF16),
        compiler_params=_cparams("parallel", "parallel"),
        name="nsa_compress",
    )(xkv, w, pos)


def _nsa_cmp_kernel(q_ref, kc_ref, vc_ref, ov_ref, o_ref, sel_ref, *, scale, nsel):
    i = pl.program_id(2)
    tq = q_ref.shape[2]
    ncmp = kc_ref.shape[2]
    kc = kc_ref[0, 0]
    vc = vc_ref[0, 0]
    ov = ov_ref[...]
    rows = NSA_R * tq
    assert tq & (tq - 1) == 0
    pos = i * tq + (lax.broadcasted_iota(jnp.int32, (rows, ncmp), 0) & (tq - 1))
    ncol = lax.broadcasted_iota(jnp.int32, (rows, ncmp), 1)
    ok = (ncol * CMP_STRIDE + (CMP_LEN - 1)) <= pos
    s = jnp.where(ok, _dot_nt(q_ref[0].reshape(rows, HEAD_DIM), kc) * scale, NEG)
    m = jnp.max(s, axis=1, keepdims=True)
    p = jnp.where(ok, jnp.exp(s - m), 0.0)
    l = jnp.sum(p, axis=1, keepdims=True)
    pn = p / jnp.where(l > 0.0, l, 1.0)
    o_ref[0] = _dot(pn.astype(BF16), vc).reshape(NSA_R, tq, HEAD_DIM)
    psum = jnp.sum(pn.reshape(NSA_R, tq, ncmp), axis=0)
    hi = psum.astype(BF16)
    lo = (psum - hi.astype(F32)).astype(BF16)
    imp = _dot(hi, ov) + _dot(lo, ov)

    imp = imp.T[:nsel]
    jj = lax.broadcasted_iota(jnp.int32, imp.shape, 0)
    own = (i * tq + lax.broadcasted_iota(jnp.int32, imp.shape, 1)) // SEL_BLOCK
    forced = (jj == 0) | (jj == own) | (jj == own - 1)
    visible = jj <= own
    rank = _rank_rows(jnp.where(visible, jnp.where(forced, NSA_FORCE, imp), NEG))
    sel_ref[0, 0] = _pad_rows_t(jnp.where((rank < SEL_TOPK) & visible, 0.0, NEG), NEG)


def _nsa_cmp(proj, kvc, overlap):
    b, _, s, _ = proj.shape
    tq = 128
    ncmp = kvc.shape[2]
    return pl.pallas_call(
        functools.partial(_nsa_cmp_kernel, scale=HEAD_DIM ** -0.5, nsel=s // SEL_BLOCK),
        grid=(b, NSA_G, s // tq),
        in_specs=[
            pl.BlockSpec((1, NSA_R, tq, HEAD_DIM), lambda bb, g, i: (bb, _NQ // NSA_R + g, i, 0)),
            pl.BlockSpec((1, 1, ncmp, HEAD_DIM), lambda bb, g, i: (bb, g, 0, 0)),
            pl.BlockSpec((1, 1, ncmp, HEAD_DIM), lambda bb, g, i: (bb, NSA_G + g, 0, 0)),
            pl.BlockSpec(overlap.shape, lambda bb, g, i: (0, 0)),
        ],
        out_specs=[
            pl.BlockSpec((1, NSA_R, tq, HEAD_DIM), lambda bb, g, i: (bb, g, i, 0)),
            pl.BlockSpec((1, 1, tq, LANES), lambda bb, g, i: (bb, g, i, 0)),
        ],
        out_shape=[
            jax.ShapeDtypeStruct((b, N_NSA, s, HEAD_DIM), F32),
            jax.ShapeDtypeStruct((b, NSA_G, s, LANES), F32),
        ],
        compiler_params=_cparams("parallel", "parallel", "parallel"),
        name="nsa_cmp",
    )(proj, kvc, kvc, overlap)


def _nsa_sel_kernel(q_ref, k_ref, v_ref, sel_ref, ex_ref, bt_ref, o_ref, s_ref, acc_ref, *, scale):
    i = pl.program_id(2)
    tq = q_ref.shape[2]
    tk = ex_ref.shape[2]
    rows = NSA_R * tq
    q = q_ref[0].reshape(rows, HEAD_DIM)
    selneg = sel_ref[0, 0].astype(BF16)
    ratio = tk // tq

    def extra(kt):
        mask = _dot(selneg, ex_ref[kt])
        return (bt_ref[0, jnp.minimum(i - ratio * kt, bt_ref.shape[1] - 1)]
                + jnp.concatenate([mask] * NSA_R, axis=0))

    out = _softmax_attend(
        q, i // ratio + 1,
        lambda kt: k_ref[0, 0, pl.ds(pl.multiple_of(kt * tk, tk), tk), :],
        lambda kt: v_ref[0, 0, pl.ds(pl.multiple_of(kt * tk, tk), tk), :],
        extra, s_ref, acc_ref, scale)
    o_ref[0] = out.reshape(NSA_R, tq, HEAD_DIM)


def _nsa_sel(proj, selneg, expand, bias_tiles):
    b, _, s, _ = proj.shape
    tq = 128
    tk = expand.shape[2]
    rows = NSA_R * tq
    return pl.pallas_call(
        functools.partial(_nsa_sel_kernel, scale=HEAD_DIM ** -0.5),
        grid=(b, NSA_G, s // tq),
        in_specs=[
            pl.BlockSpec((1, NSA_R, tq, HEAD_DIM), lambda bb, g, i: (bb, _NQ // NSA_R + g, i, 0)),
            pl.BlockSpec((1, 1, s, HEAD_DIM), lambda bb, g, i: (bb, _NKS + g, 0, 0)),
            pl.BlockSpec((1, 1, s, HEAD_DIM), lambda bb, g, i: (bb, _NVS + g, 0, 0)),
            pl.BlockSpec((1, 1, tq, LANES), lambda bb, g, i: (bb, g, i, 0)),
            pl.BlockSpec(expand.shape, lambda bb, g, i: (0, 0, 0)),
            pl.BlockSpec((1,) + bias_tiles.shape[1:], lambda bb, g, i: (g, 0, 0, 0)),
        ],
        out_specs=pl.BlockSpec((1, NSA_R, tq, HEAD_DIM), lambda bb, g, i: (bb, g, i, 0)),
        out_shape=jax.ShapeDtypeStruct((b, N_NSA, s, HEAD_DIM), F32),
        scratch_shapes=[pltpu.VMEM((s // tk, rows, tk), F32), pltpu.VMEM((rows, HEAD_DIM), F32)],
        compiler_params=_cparams("parallel", "parallel", "arbitrary"),
        name="nsa_sel",
    )(proj, proj, proj, selneg, expand, bias_tiles)


def _nsa_win_kernel(q_ref, k_ref, v_ref, bt_ref, oc_ref, os_ref, gate_ref, o_ref, *, scale, nband):
    i = pl.program_id(2)
    tq = q_ref.shape[2]
    rows = NSA_R * tq
    q = q_ref[0].reshape(rows, HEAD_DIM)
    scores, values = [], []
    for m in range(nband):
        kt = i - m
        start = pl.multiple_of(jnp.maximum(kt, 0) * tq, tq)
        s = _dot_nt(q, k_ref[0, 0, pl.ds(start, tq), :]) * scale + bt_ref[0, m]
        scores.append(jnp.where(kt >= 0, s, NEG))
        values.append(v_ref[0, 0, pl.ds(start, tq), :])
    mx = scores[0]
    for s in scores[1:]:
        mx = jnp.maximum(mx, s)
    mx = jnp.broadcast_to(jnp.max(mx, axis=1, keepdims=True), (rows, LANES))
    lsum = jnp.zeros((rows, LANES), F32)
    acc = jnp.zeros((rows, HEAD_DIM), F32)
    for s, v in zip(scores, values):
        p = jnp.exp(s - mx)
        lsum = lsum + p
        acc = acc + _dot(p.astype(BF16), v)
    ow = (acc / jnp.sum(lsum, axis=1, keepdims=True)).reshape(NSA_R, tq, HEAD_DIM)
    g = jax.nn.sigmoid(gate_ref[0, 0])
    for r in range(NSA_R):
        gc, gs, gw = (g[:, 3 * r + br:3 * r + br + 1] for br in range(3))
        o_ref[0, r] = (gc * oc_ref[0, r] + gs * os_ref[0, r] + gw * ow[r]).astype(o_ref.dtype)


def _nsa_win_combine(proj, bias_tiles, o_cmp, o_sel, gates3):
    b, _, s, _ = proj.shape
    tq = 128
    rows = NSA_R * tq
    nband = bias_tiles.shape[1]
    blk = pl.BlockSpec((1, NSA_R, tq, HEAD_DIM), lambda bb, g, i: (bb, g, i, 0))
    return pl.pallas_call(
        functools.partial(_nsa_win_kernel, scale=HEAD_DIM ** -0.5, nband=nband),
        grid=(b, NSA_G, s // tq),
        in_specs=[
            pl.BlockSpec((1, NSA_R, tq, HEAD_DIM), lambda bb, g, i: (bb, _NQ // NSA_R + g, i, 0)),
            pl.BlockSpec((1, 1, s, HEAD_DIM), lambda bb, g, i: (bb, _NKW + g, 0, 0)),
            pl.BlockSpec((1, 1, s, HEAD_DIM), lambda bb, g, i: (bb, _NVW + g, 0, 0)),
            pl.BlockSpec((1, nband, rows, tq), lambda bb, g, i: (g, 0, 0, 0)),
            blk, blk,
            pl.BlockSpec((1, 1, tq, LANES), lambda bb, g, i: (bb, g, i, 0)),
        ],
        out_specs=blk,
        out_shape=jax.ShapeDtypeStruct((b, N_NSA, s, HEAD_DIM), BF16),
        compiler_params=_cparams("parallel", "parallel", "parallel"),
        name="nsa_win",
    )(proj, proj, proj, bias_tiles, o_cmp, o_sel, gates3)


def _outproj_kernel(om_ref, od_ref, on_ref, w_ref, x_ref, o_ref, cat_ref):
    h = 0
    for ref in (om_ref, od_ref, on_ref):
        for hh in range(ref.shape[1]):
            cat_ref[:, h * HEAD_DIM:(h + 1) * HEAD_DIM] = ref[0, hh]
            h += 1
    o_ref[...] = x_ref[...] + _dot(cat_ref[...], w_ref[...])


def _outproj(o_moba, o_diff, o_nsa, w, x2):
    n, d = x2.shape
    b, _, s, _ = o_moba.shape
    tm = 512
    spt = s // tm

    def hspec(nh):
        return pl.BlockSpec((1, nh, tm, HEAD_DIM), lambda i: (i // spt, 0, i % spt, 0))

    return pl.pallas_call(
        _outproj_kernel,
        grid=(n // tm,),
        in_specs=[hspec(N_MOBA), hspec(N_DIFF), hspec(N_NSA),
                  pl.BlockSpec(w.shape, lambda i: (0, 0)),
                  pl.BlockSpec((tm, d), lambda i: (i, 0))],
        out_specs=pl.BlockSpec((tm, d), lambda i: (i, 0)),
        out_shape=jax.ShapeDtypeStruct((n, d), F32),
        scratch_shapes=[pltpu.VMEM((tm, w.shape[0]), BF16)],
        compiler_params=_cparams("parallel"),
        name="outproj",
    )(o_moba, o_diff, o_nsa, w, x2)


def _peer_query_kernel(x_ref, g_ref, wq_ref, sk_ref, sc_ref, xt_ref):
    x = x_ref[...]
    y = x * lax.rsqrt(jnp.mean(x * x, axis=-1, keepdims=True) + NORM_EPS)
    xn = y * g_ref[...]
    xt_ref[...] = xn.T.astype(BF16)
    qry = _dot(xn.astype(BF16), wq_ref[...]).astype(BF16)
    for hp in range(sk_ref.shape[0]):
        sc_ref[hp] = _dot_nt(sk_ref[hp], qry[:, hp * PEER_NKEYS:(hp + 1) * PEER_NKEYS])


def _peer_query(x2, gain, wq, subkeys):
    n, d = x2.shape
    tm = 512
    nhp = subkeys.shape[0]
    return pl.pallas_call(
        _peer_query_kernel,
        grid=(n // tm,),
        in_specs=[
            pl.BlockSpec((tm, d), lambda i: (i, 0)),
            pl.BlockSpec((1, d), lambda i: (0, 0)),
            pl.BlockSpec(wq.shape, lambda i: (0, 0)),
            pl.BlockSpec(subkeys.shape, lambda i: (0, 0, 0)),
        ],
        out_specs=[
            pl.BlockSpec((nhp, PEER_NKEYS, tm), lambda i: (0, 0, i)),
            pl.BlockSpec((d, tm), lambda i: (0, i)),
        ],
        out_shape=[
            jax.ShapeDtypeStruct((nhp, PEER_NKEYS, n), F32),
            jax.ShapeDtypeStruct((d, n), BF16),
        ],
        compiler_params=_cparams("parallel"),
        name="peer_query",
    )(x2, gain, wq, subkeys)


def _top_values(a, k, exact_ties):
    rowid = lax.broadcasted_iota(jnp.int32, a.shape, 0)
    spare = jnp.sum(jnp.where(a == NEG, 1.0, 0.0), axis=0, keepdims=True)
    start = a
    member = jnp.zeros(a.shape, F32)
    vals = []
    for _ in range(k):
        m = jnp.max(a, axis=0, keepdims=True)
        hit = a == m
        if exact_ties:
            first = jnp.min(jnp.where(hit, rowid, a.shape[0]), axis=0, keepdims=True)
            hit = rowid == first
            member = jnp.where(hit, 1.0, member)
        a = jnp.where(hit, NEG, a)
        vals.append(m)
    if not exact_ties:
        member = jnp.where(start >= vals[-1], 1.0, 0.0)
    removed = jnp.sum(jnp.where(a == NEG, 1.0, 0.0), axis=0, keepdims=True) - spare
    return vals, member, jnp.where(removed > k, 1.0, 0.0)


def _route_head(a, b, exact_ties):
    k = PEER_TOPK
    av, am, fa = _top_values(a, k, exact_ties)
    bv, bm, fb = _top_values(b, k, exact_ties)
    pairs = [(r, c) for r in range(k) for c in range(k) if (r + 1) * (c + 1) <= k + 1]
    nrow = -(-len(pairs) // 8) * 8
    crow = lax.broadcasted_iota(jnp.int32, (nrow, a.shape[1]), 0)
    cmat = jnp.full((nrow, a.shape[1]), NEG, F32)
    for idx, (r, c) in enumerate(pairs):
        cmat = jnp.where(crow == idx, av[r] + bv[c], cmat)
    top, _, fc = _top_values(cmat, k + 1, exact_ties)
    peak = top[0]
    z = jnp.zeros_like(peak)
    for t in top[:k]:
        z = z + jnp.exp(t - peak)
    mid = 0.5 * (top[k - 1] + top[k])
    w1 = jnp.where(am > 0.0, jnp.exp(a - av[0]), 0.0)
    w2 = jnp.where(bm > 0.0, jnp.exp(b - bv[0]), 0.0) / z
    return w1, w2, jnp.exp(mid - peak) / z, jnp.maximum(jnp.maximum(fa, fb), fc)


def _peer_route_kernel(sc_ref, w1_ref, w2_ref, th_ref):
    def route(h, exact_ties):
        w1, w2, th, flag = _route_head(sc_ref[2 * h], sc_ref[2 * h + 1], exact_ties)
        w1_ref[h] = w1
        w2_ref[h] = w2
        th_ref[h:h + 1, :] = th
        return flag

    for h in range(PEER_HEADS):
        flag = route(h, False)

        @pl.when(jnp.max(flag) > 0.0)
        def _(h=h):
            route(h, True)


def _peer_route(sc):
    nhp, nk, n = sc.shape
    t = 256
    return pl.pallas_call(
        _peer_route_kernel,
        grid=(n // t,),
        in_specs=[pl.BlockSpec((nhp, nk, t), lambda i: (0, 0, i))],
        out_specs=[
            pl.BlockSpec((PEER_HEADS, nk, t), lambda i: (0, 0, i)),
            pl.BlockSpec((PEER_HEADS, nk, t), lambda i: (0, 0, i)),
            pl.BlockSpec((PEER_HEADS, t), lambda i: (0, i)),
        ],
        out_shape=[
            jax.ShapeDtypeStruct((PEER_HEADS, nk, n), F32),
            jax.ShapeDtypeStruct((PEER_HEADS, nk, n), F32),
            jax.ShapeDtypeStruct((PEER_HEADS, n), F32),
        ],
        compiler_params=_cparams("parallel"),
        name="peer_route",
    )(sc)


def _peer_expert_kernel(xt_ref, u_ref, vt_ref, w1_ref, w2_ref, th_ref, o_ref, a_ref, *, rows_per_tile, n_tiles):
    e = pl.program_id(1)

    @pl.when(e == 0)
    def _():
        o_ref[...] = jnp.zeros_like(o_ref)
        a_ref[...] = jnp.zeros_like(a_ref)

    slot = e % 2
    row0 = jnp.minimum(e, n_tiles - 1) * rows_per_tile
    a_prev = a_ref[1 - slot]
    orows = o_ref.shape[0] // rows_per_tile
    for ii in range(rows_per_tile):
        hv = _dot(u_ref[ii * PEER_NKEYS:(ii + 1) * PEER_NKEYS, :], xt_ref[...])
        o_ref[ii * orows:(ii + 1) * orows, :] += _dot(vt_ref[ii * orows:(ii + 1) * orows, :], a_prev)
        wrows = [w1_ref[hd, pl.ds(row0 + ii, 1), :] for hd in range(PEER_HEADS)]
        for lt in range(xt_ref.shape[1] // LANES):
            ls = slice(lt * LANES, (lt + 1) * LANES)
            g = None
            for hd in range(PEER_HEADS):
                p = wrows[hd][:, ls] * w2_ref[hd, :, ls]
                term = jnp.where(p >= th_ref[hd:hd + 1, ls], p, 0.0)
                g = term if g is None else g + term
            hl = hv[:, ls]
            gelu = 0.5 * hl * (1.0 + lax.erf(hl * (2.0 ** -0.5)))
            a_ref[slot, ii * PEER_NKEYS:(ii + 1) * PEER_NKEYS, ls] = (gelu * g).astype(BF16)


def _peer_experts(xt, u, vt, w1, w2, th, layer):
    d, n = xt.shape
    ne = u.shape[1]
    t = min(512, n)
    rpt = 4
    et = rpt * PEER_NKEYS
    n_tiles = ne // et
    return pl.pallas_call(
        functools.partial(_peer_expert_kernel, rows_per_tile=rpt, n_tiles=n_tiles),
        grid=(n // t, n_tiles + 1),
        in_specs=[
            pl.BlockSpec((d, t), lambda i, e: (0, i)),
            pl.BlockSpec((None, et, d), lambda i, e: (layer, jnp.minimum(e, n_tiles - 1), 0)),
            pl.BlockSpec((None, d, et), lambda i, e: (layer, 0, jnp.maximum(e - 1, 0))),
            pl.BlockSpec((PEER_HEADS, PEER_NKEYS, t), lambda i, e: (0, 0, i)),
            pl.BlockSpec((PEER_HEADS, PEER_NKEYS, t), lambda i, e: (0, 0, i)),
            pl.BlockSpec((PEER_HEADS, t), lambda i, e: (0, i)),
        ],
        out_specs=pl.BlockSpec((d, t), lambda i, e: (0, i)),
        out_shape=jax.ShapeDtypeStruct((d, n), F32),
        scratch_shapes=[pltpu.VMEM((2, et, t), BF16)],
        compiler_params=_cparams("parallel", "arbitrary"),
        name="peer_experts",
    )(xt, u, vt, w1, w2, th)


def _residual_t_kernel(x_ref, yt_ref, o_ref):
    o_ref[...] = x_ref[...] + yt_ref[...].T


def _residual_t(x2, yt):
    n, d = x2.shape
    tm = 512
    return pl.pallas_call(
        _residual_t_kernel,
        grid=(n // tm,),
        in_specs=[pl.BlockSpec((tm, d), lambda i: (i, 0)), pl.BlockSpec((d, tm), lambda i: (0, i))],
        out_specs=pl.BlockSpec((tm, d), lambda i: (i, 0)),
        out_shape=jax.ShapeDtypeStruct((n, d), F32),
        compiler_params=_cparams("parallel"),
        name="residual_t",
    )(x2, yt)


def _final_norm_kernel(x_ref, g_ref, o_ref):
    x = x_ref[...]
    y = x * lax.rsqrt(jnp.mean(x * x, axis=-1, keepdims=True) + NORM_EPS)
    o_ref[...] = y * g_ref[...]


def _final_norm(x2, gain):
    n, d = x2.shape
    tm = 512
    return pl.pallas_call(
        _final_norm_kernel,
        grid=(n // tm,),
        in_specs=[pl.BlockSpec((tm, d), lambda i: (i, 0)), pl.BlockSpec((1, d), lambda i: (0, 0))],
        out_specs=pl.BlockSpec((tm, d), lambda i: (i, 0)),
        out_shape=jax.ShapeDtypeStruct((n, d), F32),
        compiler_params=_cparams("parallel"),
        name="final_norm",
    )(x2, gain)


def _position_tables(rel_bias, seq):
    bias_moba = rel_bias[:, :N_MOBA]
    bias_diff = rel_bias[:, N_MOBA:N_MOBA + N_DIFF]
    bias_nsa = rel_bias[:, N_MOBA + N_DIFF:]
    blk = MOBA_BLOCK
    offs = tuple(blk * m for m in range(FLASH_TK // blk + 2))
    tabs = {
        "moba": _bias_tiles(bias_moba, blk, FLASH_TK, offs, scale=HEAD_DIM ** -0.5),
        "diff": _bias_tiles(bias_diff, blk, FLASH_TK, offs, scale=(HEAD_DIM // 2) ** -0.5),
    }
    tq, tk = NSA_TQ, NSA_SEL_TK
    n_off = tk // tq + 2
    bt_sel = _bias_tiles(bias_nsa, tq, tk, tuple(tq * m for m in range(n_off)), scale=HEAD_DIM ** -0.5)
    tabs["sel"] = bt_sel.reshape(NSA_G, NSA_R, n_off, tq, tk).transpose(0, 2, 1, 3, 4).reshape(
        NSA_G, n_off, NSA_R * tq, tk)
    nband = NSA_WINDOW // tq + 1
    bt_win = _bias_tiles(bias_nsa, tq, tq, tuple(tq * m for m in range(nband)), lo=0, hi=NSA_WINDOW)
    tabs["win"] = bt_win.reshape(NSA_G, NSA_R, nband, tq, tq).transpose(0, 2, 1, 3, 4).reshape(
        NSA_G, nband, NSA_R * tq, tq)

    nrow = seq // CMP_STRIDE
    n_sel = seq // SEL_BLOCK
    cmp_start = jnp.arange(nrow) * CMP_STRIDE
    sel_start = jnp.arange(LANES) * SEL_BLOCK
    tabs["overlap"] = ((cmp_start[:, None] <= (sel_start + SEL_BLOCK - 1)[None, :])
                       & ((cmp_start + CMP_LEN - 1)[:, None] >= sel_start[None, :])
                       & (jnp.arange(nrow) < (seq - CMP_LEN) // CMP_STRIDE + 1)[:, None]
                       & (jnp.arange(LANES) < n_sel)[None, :]).astype(BF16)
    expand = (jnp.arange(LANES)[:, None] == (jnp.arange(seq) // SEL_BLOCK)[None, :]).astype(BF16)
    tabs["expand"] = expand.reshape(LANES, seq // tk, tk).transpose(1, 0, 2)
    return tabs


def _attention_block(x2, batch, seq, tabs, attn_norm, w_in, diff_lambda, diff_subln, cmp_pos, cmp_w, w_out,
                     lam_init):
    d = x2.shape[1]
    n_main = _N_PROJ_HEADS * HEAD_DIM
    w_main = w_in[:, :n_main].astype(BF16)
    w_gate = jnp.pad(w_in[:, n_main:], ((0, 0), (0, LANES - (w_in.shape[1] - n_main)))).astype(BF16)
    proj, gates = _inproj(x2, attn_norm.reshape(1, d), w_main, w_gate, batch, seq)

    o_moba = _moba(proj, tabs["moba"])
    o_diff = _diff(proj, tabs["diff"], diff_lambda, diff_subln.reshape(1, HEAD_DIM), lam_init)

    nrow = seq // CMP_STRIDE
    wid = CMP_STRIDE * HEAD_DIM
    xkv = proj[:, _NKC:_NKC + 2 * NSA_G].reshape(batch, 2 * NSA_G, nrow, wid)
    w_cmp = cmp_w.reshape(2, 2, wid, HEAD_DIM).astype(BF16)
    pos = jnp.pad(cmp_pos.reshape(2, 2, wid), ((0, 0), (0, 6), (0, 0))).astype(BF16)
    kvc = _nsa_compress(xkv, w_cmp, pos)

    o_cmp, selneg = _nsa_cmp(proj, kvc, tabs["overlap"])
    o_sel = _nsa_sel(proj, selneg, tabs["expand"], tabs["sel"])

    n_gate = N_NSA * 3
    gates_g = gates[:, :n_gate].reshape(batch, seq, NSA_G, n_gate // NSA_G).transpose(0, 2, 1, 3)
    gates_g = jnp.pad(gates_g, ((0, 0), (0, 0), (0, 0), (0, LANES - n_gate // NSA_G)))
    o_nsa = _nsa_win_combine(proj, tabs["win"], o_cmp, o_sel, gates_g)

    return _outproj(o_moba, o_diff, o_nsa, w_out.astype(BF16), x2)


def _peer_block(x2, ffn_norm, wq, subkeys, u_all, vt_all, layer):
    d = x2.shape[1]
    sk = subkeys.reshape(PEER_HEADS * 2, PEER_NKEYS, subkeys.shape[-1]).astype(BF16)
    sc, xt = _peer_query(x2, ffn_norm.reshape(1, d), wq.astype(BF16), sk)
    w1, w2, th = _peer_route(sc)
    yt = _peer_experts(xt, u_all, vt_all, w1, w2, th, layer)
    return _residual_t(x2, yt)


def kernel(x, rel_bias, attn_norm, w_in, diff_lambda, diff_subln, nsa_cmp_pos, nsa_cmp_w, w_out, ffn_norm,
           peer_wq, peer_subkeys, peer_u, peer_v, final_norm):
    batch, seq, d = x.shape
    x2 = x.reshape(batch * seq, d)
    tabs = _position_tables(rel_bias, seq)
    u_all = peer_u.astype(BF16)
    vt_all = jnp.swapaxes(peer_v.astype(BF16), 1, 2)
    for l in range(attn_norm.shape[0]):
        lam_init = 0.8 - 0.6 * math.exp(-0.3 * l)
        x2 = _attention_block(x2, batch, seq, tabs, attn_norm[l], w_in[l], diff_lambda[l], diff_subln[l],
                              nsa_cmp_pos[l], nsa_cmp_w[l], w_out[l], lam_init)
        x2 = _peer_block(x2, ffn_norm[l], peer_wq[l], peer_subkeys[l], u_all, vt_all, l)
    return _final_norm(x2, final_norm.reshape(1, d)).reshape(batch, seq, d)
```
